```python
import jax, jax.numpy as jnp
from jax import lax
import numpy as np

D_MODEL = 4096
BATCH = 2
SEQ = 4096
DEPTH = 2

F32 = jnp.float32
EPS = 1e-6
NEG_INF = -1e30

MIX_WIDTH = D_MODEL
N_GROUPS = 4
GROUP_WIDTH = MIX_WIDTH // N_GROUPS

MLA_NOPE = 128
MLA_ROPE = 64
MLA_V = 128
MLA_HEADS = GROUP_WIDTH // MLA_V
MLA_QK = MLA_NOPE + MLA_ROPE
MLA_Q_LORA = 1024
MLA_KV_LORA = 512
ROPE_THETA = 10000.0
ATTN_BLOCK = 128

POOL_WINDOWS = (2, 4, 8, 16)
POOL_GROUP = GROUP_WIDTH // len(POOL_WINDOWS)

CONV_WIDTH = 31

SWA_HEAD_DIM = 64
SWA_Q_HEADS = GROUP_WIDTH // SWA_HEAD_DIM
SWA_KV_HEADS = SWA_Q_HEADS // 8
SWA_WINDOW = 128
SWA_BLOCK = SWA_WINDOW

D_FF = 11008
N_EXPERTS = 8
TOP_K = 2
D_FF_EXPERT = D_FF // 2
N_DENSE = (DEPTH + 1) // 2
N_MOE = DEPTH // 2

MLA_IN = MLA_Q_LORA + MLA_KV_LORA + MLA_ROPE
POOL_IN = GROUP_WIDTH
CONV_IN = 2 * GROUP_WIDTH
SWA_IN = SWA_Q_HEADS * SWA_HEAD_DIM + 2 * SWA_KV_HEADS * SWA_HEAD_DIM
OFF_POOL = MLA_IN
OFF_CONV = OFF_POOL + POOL_IN
OFF_SWA = OFF_CONV + CONV_IN
IN_WIDTH = OFF_SWA + SWA_IN

kernel_name = 'hybrid_mla_pool_conv_swa_moe_block'


def rms_norm(x, g):
    xf = x.astype(F32)
    y = xf * lax.rsqrt(jnp.mean(xf * xf, axis=-1, keepdims=True) + EPS)
    return (y * g.astype(F32)).astype(x.dtype)


def layer_norm(x, g, b):
    xf = x.astype(F32)
    mu = jnp.mean(xf, axis=-1, keepdims=True)
    xc = xf - mu
    y = xc * lax.rsqrt(jnp.mean(xc * xc, axis=-1, keepdims=True) + EPS)
    return (y * g.astype(F32) + b.astype(F32)).astype(x.dtype)


def rope_tables(seq):
    inv = 1.0 / (ROPE_THETA ** (jnp.arange(0, MLA_ROPE, 2, dtype=F32) / MLA_ROPE))
    ang = jnp.arange(seq, dtype=F32)[:, None] * inv[None, :]
    return jnp.cos(ang), jnp.sin(ang)


def apply_rope(x, cos, sin):
    half = x.shape[-1] // 2
    x1 = x[..., :half].astype(F32)
    x2 = x[..., half:].astype(F32)
    c = cos[None, :, None, :]
    s = sin[None, :, None, :]
    return jnp.concatenate([x1 * c - x2 * s, x2 * c + x1 * s], axis=-1).astype(x.dtype)


def mla_mixer(u, q_norm_g, w_q_up, kv_norm_g, w_kv_up, cos, sin):
    B, S, _ = u.shape
    c_q = u[..., :MLA_Q_LORA]
    c_kv = u[..., MLA_Q_LORA:MLA_Q_LORA + MLA_KV_LORA]
    k_pe = u[..., MLA_Q_LORA + MLA_KV_LORA:]
    q = (rms_norm(c_q, q_norm_g) @ w_q_up).reshape(B, S, MLA_HEADS, MLA_QK)
    q = jnp.concatenate([q[..., :MLA_NOPE], apply_rope(q[..., MLA_NOPE:], cos, sin)], axis=-1)
    kv = (rms_norm(c_kv, kv_norm_g) @ w_kv_up).reshape(B, S, MLA_HEADS, MLA_NOPE + MLA_V)
    k_nope, v = kv[..., :MLA_NOPE], kv[..., MLA_NOPE:]
    k_pe = apply_rope(k_pe[:, :, None, :], cos, sin)
    k = jnp.concatenate([k_nope, jnp.broadcast_to(k_pe, (B, S, MLA_HEADS, MLA_ROPE))], axis=-1)
    scale = MLA_QK ** -0.5
    nblk = S // ATTN_BLOCK
    q_blocks = q.reshape(B, nblk, ATTN_BLOCK, MLA_HEADS, MLA_QK).transpose(1, 0, 2, 3, 4)
    k_pos = jnp.arange(S)

    def attend(args):
        q_blk, start = args
        s = jnp.einsum('bqhd,bkhd->bhqk', q_blk, k).astype(F32) * scale
        q_pos = start + jnp.arange(ATTN_BLOCK)
        s = jnp.where(q_pos[:, None] >= k_pos[None, :], s, NEG_INF)
        p = jax.nn.softmax(s, axis=-1).astype(v.dtype)
        return jnp.einsum('bhqk,bkhd->bqhd', p, v)

    starts = jnp.arange(nblk) * ATTN_BLOCK
    o = lax.map(attend, (q_blocks, starts))
    return o.transpose(1, 0, 2, 3, 4).reshape(B, S, MLA_HEADS * MLA_V)


def pool_mixer(u, w_pool, pool_scale):
    B, S, C = u.shape
    uf = u.astype(F32)
    csum = jnp.concatenate([jnp.zeros((B, 1, C), F32), jnp.cumsum(uf, axis=1)], axis=1)
    t = jnp.arange(S)
    outs = []
    for gi, w in enumerate(POOL_WINDOWS):
        lo_c, hi_c = gi * POOL_GROUP, (gi + 1) * POOL_GROUP
        c = csum[..., lo_c:hi_c]
        lo = jnp.maximum(t + 1 - w, 0)
        win_sum = c[:, 1:] - c[:, lo]
        count = jnp.minimum(t + 1, w).astype(F32)[None, :, None]
        outs.append(win_sum / count - uf[..., lo_c:hi_c])
    d = jnp.stack(outs, axis=2).astype(u.dtype)
    y = jnp.einsum('bsgc,gcd->bsgd', d, w_pool).reshape(B, S, C)
    return y * pool_scale


def conv_mixer(u, w_dw, b_dw, ln_g, ln_b, w_pw):
    a, gate = u[..., :GROUP_WIDTH], u[..., GROUP_WIDTH:]
    z = a * jax.nn.sigmoid(gate)
    z = lax.conv_general_dilated(
        z, w_dw[:, None, :], window_strides=(1,), padding=[(CONV_WIDTH - 1, 0)],
        dimension_numbers=('NWC', 'WIO', 'NWC'), feature_group_count=GROUP_WIDTH) + b_dw
    z = jax.nn.silu(layer_norm(z, ln_g, ln_b))
    return z @ w_pw


def swa_mixer(u, sinks):
    B, S, _ = u.shape
    nblk = S // SWA_BLOCK
    R = SWA_Q_HEADS // SWA_KV_HEADS
    qd = SWA_Q_HEADS * SWA_HEAD_DIM
    kd = SWA_KV_HEADS * SWA_HEAD_DIM
    q = u[..., :qd].reshape(B, nblk, SWA_BLOCK, SWA_KV_HEADS, R, SWA_HEAD_DIM)
    k = u[..., qd:qd + kd].reshape(B, nblk, SWA_BLOCK, SWA_KV_HEADS, SWA_HEAD_DIM)
    v = u[..., qd + kd:].reshape(B, nblk, SWA_BLOCK, SWA_KV_HEADS, SWA_HEAD_DIM)

    def band(t):
        prev = jnp.concatenate([jnp.zeros_like(t[:, :1]), t[:, :-1]], axis=1)
        return jnp.concatenate([prev, t], axis=2)

    kb, vb = band(k), band(v)
    s = jnp.einsum('bnqgrd,bnkgd->bngrqk', q, kb).astype(F32) * (SWA_HEAD_DIM ** -0.5)
    qi = jnp.arange(SWA_BLOCK)[:, None]
    kj = jnp.arange(2 * SWA_BLOCK)[None, :]
    rel = qi + SWA_BLOCK - kj
    blk_start = (jnp.arange(nblk) * SWA_BLOCK)[:, None, None]
    valid = (rel >= 0) & (rel < SWA_WINDOW) & (blk_start + kj[None] - SWA_BLOCK >= 0)
    s = jnp.where(valid[None, :, None, None], s, NEG_INF)
    sink = sinks.astype(F32).reshape(1, 1, SWA_KV_HEADS, R, 1, 1)
    lse = jnp.logaddexp(jax.nn.logsumexp(s, axis=-1, keepdims=True), sink)
    p = jnp.exp(s - lse).astype(vb.dtype)
    o = jnp.einsum('bngrqk,bnkgd->bnqgrd', p, vb)
    return o.reshape(B, S, qd)


def swiglu(h, w_gate, w_up, w_down):
    return (jax.nn.silu(h @ w_gate) * (h @ w_up)) @ w_down


def moe_ffn(h, w_router, w_gate, w_up, w_down):
    B, S, D = h.shape
    t = h.reshape(B * S, D)
    logits = (t @ w_router).astype(F32)
    top_val, top_idx = lax.top_k(logits, TOP_K)
    top_w = jax.nn.softmax(top_val, axis=-1)
    gates = jnp.sum(jax.nn.one_hot(top_idx, N_EXPERTS, dtype=F32) * top_w[..., None], axis=1)
    y = jnp.zeros_like(t)
    for e in range(N_EXPERTS):
        y = y + gates[:, e:e + 1].astype(t.dtype) * swiglu(t, w_gate[e], w_up[e], w_down[e])
    return y.reshape(B, S, D)


def setup_inputs(seed: int = 0) -> dict:
    key = jax.random.key(seed)
    ks = iter(jax.random.split(key, 32))

    def nrm(shape, scale):
        return jax.random.normal(next(ks), shape, F32) * scale

    def gain(shape):
        return 1.0 + 0.05 * jax.random.normal(next(ks), shape, F32)

    L = DEPTH
    return {
        'x': jax.random.normal(next(ks), (BATCH, SEQ, D_MODEL), F32),
        'attn_norm_g': gain((L, D_MODEL)),
        'w_in': nrm((L, D_MODEL, IN_WIDTH), D_MODEL ** -0.5),
        'mla_q_norm_g': gain((L, MLA_Q_LORA)),
        'mla_w_q_up': nrm((L, MLA_Q_LORA, MLA_HEADS * MLA_QK), MLA_Q_LORA ** -0.5),
        'mla_kv_norm_g': gain((L, MLA_KV_LORA)),
        'mla_w_kv_up': nrm((L, MLA_KV_LORA, MLA_HEADS * (MLA_NOPE + MLA_V)), MLA_KV_LORA ** -0.5),
        'pool_w': nrm((L, len(POOL_WINDOWS), POOL_GROUP, POOL_GROUP), POOL_GROUP ** -0.5),
        'pool_scale': gain((L, GROUP_WIDTH)),
        'conv_w_dw': nrm((L, CONV_WIDTH, GROUP_WIDTH), CONV_WIDTH ** -0.5),
        'conv_b_dw': nrm((L, GROUP_WIDTH), 0.02),
        'conv_ln_g': gain((L, GROUP_WIDTH)),
        'conv_ln_b': nrm((L, GROUP_WIDTH), 0.02),
        'conv_w_pw': nrm((L, GROUP_WIDTH, GROUP_WIDTH), GROUP_WIDTH ** -0.5),
        'swa_sinks': nrm((L, SWA_Q_HEADS), 0.5),
        'group_out_g': gain((L, MIX_WIDTH)),
        'w_out': nrm((L, MIX_WIDTH, D_MODEL), MIX_WIDTH ** -0.5),
        'ffn_norm_g': gain((L, D_MODEL)),
        'dense_w_gate': nrm((N_DENSE, D_MODEL, D_FF), D_MODEL ** -0.5),
        'dense_w_up': nrm((N_DENSE, D_MODEL, D_FF), D_MODEL ** -0.5),
        'dense_w_down': nrm((N_DENSE, D_FF, D_MODEL), D_FF ** -0.5),
        'moe_w_router': nrm((N_MOE, D_MODEL, N_EXPERTS), D_MODEL ** -0.5),
        'moe_w_gate': nrm((N_MOE, N_EXPERTS, D_MODEL, D_FF_EXPERT), D_MODEL ** -0.5),
        'moe_w_up': nrm((N_MOE, N_EXPERTS, D_MODEL, D_FF_EXPERT), D_MODEL ** -0.5),
        'moe_w_down': nrm((N_MOE, N_EXPERTS, D_FF_EXPERT, D_MODEL), D_FF_EXPERT ** -0.5),
        'final_norm_g': gain((D_MODEL,)),
    }


def reference(x, attn_norm_g, w_in, mla_q_norm_g, mla_w_q_up, mla_kv_norm_g, mla_w_kv_up,
              pool_w, pool_scale, conv_w_dw, conv_b_dw, conv_ln_g, conv_ln_b, conv_w_pw,
              swa_sinks, group_out_g, w_out, ffn_norm_g, dense_w_gate, dense_w_up, dense_w_down,
              moe_w_router, moe_w_gate, moe_w_up, moe_w_down, final_norm_g):
    B, S, _ = x.shape
    cos, sin = rope_tables(S)
    for l in range(DEPTH):
        h = rms_norm(x, attn_norm_g[l])
        u = h @ w_in[l]
        y_a = mla_mixer(u[..., :OFF_POOL], mla_q_norm_g[l], mla_w_q_up[l],
                        mla_kv_norm_g[l], mla_w_kv_up[l], cos, sin)
        y_b = pool_mixer(u[..., OFF_POOL:OFF_CONV], pool_w[l], pool_scale[l])
        y_c = conv_mixer(u[..., OFF_CONV:OFF_SWA], conv_w_dw[l], conv_b_dw[l],
                         conv_ln_g[l], conv_ln_b[l], conv_w_pw[l])
        y_d = swa_mixer(u[..., OFF_SWA:], swa_sinks[l])
        y = jnp.stack([y_a, y_b, y_c, y_d], axis=2)
        y = rms_norm(y, group_out_g[l].reshape(N_GROUPS, GROUP_WIDTH)).reshape(B, S, MIX_WIDTH)
        x = x + y @ w_out[l]
        h = rms_norm(x, ffn_norm_g[l])
        if l % 2 == 0:
            i = l // 2
            x = x + swiglu(h, dense_w_gate[i], dense_w_up[i], dense_w_down[i])
        else:
            i = l // 2
            x = x + moe_ffn(h, moe_w_router[i], moe_w_gate[i], moe_w_up[i], moe_w_down[i])
    return rms_norm(x, final_norm_g)
```

```python
import functools

import jax
import jax.numpy as jnp
from jax import lax
from jax.experimental import pallas as pl
from jax.experimental.pallas import tpu as pltpu

F32 = jnp.float32
BF16 = jnp.bfloat16
EPS = 1e-6
NEG_INF = -1e30

GROUP_WIDTH = 1024
MLA_NOPE = 128
MLA_ROPE = 64
MLA_V = 128
MLA_HEADS = 8
MLA_QK = MLA_NOPE + MLA_ROPE
MLA_Q_LORA = 1024
MLA_KV_LORA = 512
MLA_HEAD_PAD = 256
ROPE_THETA = 10000.0
POOL_WINDOWS = (2, 4, 8, 16)
POOL_GROUP = 256
POOL_HALO = 16
CONV_WIDTH = 31
CONV_HALO = 32
SWA_HEAD_DIM = 64
SWA_Q_HEADS = 16
SWA_KV_HEADS = 2
SWA_WINDOW = 128
N_EXPERTS = 8

U_POOL = 0
U_CONV_A = 1024
U_CONV_G = 2048
U_SWA_Q = 3072
U_CQ = 4096
U_CKV = 5120
U_KPE = 5632
U_SWA_K = 5760
U_SWA_V = 5888
U_WIDTH = 6144

LANE = 128
MOE_TILE = 1024
MOE_SUB = 256
VMEM_LIMIT = 56 * 1024 * 1024


def _params(sem, vmem=VMEM_LIMIT):
    return pltpu.CompilerParams(dimension_semantics=sem, vmem_limit_bytes=vmem)


def _rms(x, g):
    return x * lax.rsqrt(jnp.mean(x * x, axis=-1, keepdims=True) + EPS) * g


def _norm_kernel(*refs, has_delta, write_sum, out_dtype):
    it = iter(refs)
    x_ref = next(it)
    d_ref = next(it) if has_delta else None
    g_ref = next(it)
    s_ref = next(it) if write_sum else None
    o_ref = next(it)
    x = x_ref[...]
    if has_delta:
        x = x + d_ref[...].astype(F32)
    if write_sum:
        s_ref[...] = x
    o_ref[...] = _rms(x, g_ref[...]).astype(out_dtype)


def _norm(x, g, delta=None, write_sum=False, out_dtype=BF16, tm=256):
    T, D = x.shape
    row = pl.BlockSpec((tm, D), lambda i: (i, 0))
    in_specs = [row] + ([row] if delta is not None else []) + [pl.BlockSpec((1, D), lambda i: (0, 0))]
    args = [x] + ([delta] if delta is not None else []) + [g.reshape(1, D).astype(F32)]
    out_shape = [jax.ShapeDtypeStruct((T, D), out_dtype)]
    out_specs = [row]
    if write_sum:
        out_shape = [jax.ShapeDtypeStruct((T, D), F32)] + out_shape
        out_specs = [row] + out_specs
    res = pl.pallas_call(
        functools.partial(_norm_kernel, has_delta=delta is not None, write_sum=write_sum, out_dtype=out_dtype),
        grid=(T // tm,), in_specs=in_specs, out_specs=out_specs, out_shape=out_shape,
        compiler_params=_params(("parallel",)), name="norm")(*args)
    return res if write_sum else res[0]


def _mm_kernel(*refs, nx, has_res):
    x_refs = refs[:nx]
    w_ref = refs[nx]
    res_ref = refs[nx + 1] if has_res else None
    o_ref = refs[-1]
    acc = None
    off = 0
    for xr in x_refs:
        kx = xr.shape[1]
        p = jnp.dot(xr[...], w_ref[off:off + kx, :].astype(BF16), preferred_element_type=F32)
        acc = p if acc is None else acc + p
        off += kx
    if has_res:
        acc = acc + res_ref[...]
    o_ref[...] = acc.astype(o_ref.dtype)


def _mm(xs, w3, g, tm, tn, out_dtype, res=None):
    M = xs[0].shape[0]
    _, K, N = w3.shape
    assert sum(x.shape[1] for x in xs) == K
    in_specs = [pl.BlockSpec((tm, x.shape[1]), lambda m, n: (m, 0)) for x in xs]
    in_specs.append(pl.BlockSpec((None, K, tn), lambda m, n: (g, 0, n)))
    args = list(xs) + [w3]
    if res is not None:
        in_specs.append(pl.BlockSpec((tm, tn), lambda m, n: (m, n)))
        args.append(res)
    return pl.pallas_call(
        functools.partial(_mm_kernel, nx=len(xs), has_res=res is not None),
        grid=(M // tm, N // tn), in_specs=in_specs,
        out_specs=pl.BlockSpec((tm, tn), lambda m, n: (m, n)),
        out_shape=jax.ShapeDtypeStruct((M, N), out_dtype),
        compiler_params=_params(("parallel", "arbitrary")), name="mm")(*args)


def _swiglu_kernel(te_ref, ns_ref, src_ref, x_ref, wg_ref, wu_ref, o_ref, *, nsub, sub):
    ns = ns_ref[pl.program_id(0)]
    for v in range(nsub + 1):
        @pl.when(ns == v)
        def _(v=v):
            rows = v * sub
            if v > 0:
                x = x_ref[:rows, :]
                g = jnp.dot(x, wg_ref[...].astype(BF16), preferred_element_type=F32)
                u = jnp.dot(x, wu_ref[...].astype(BF16), preferred_element_type=F32)
                o_ref[:rows, :] = (g * jax.nn.sigmoid(g) * u).astype(o_ref.dtype)
            if v < nsub:
                o_ref[rows:, :] = jnp.zeros((nsub * sub - rows, o_ref.shape[1]), o_ref.dtype)


def _swiglu(x, wg, wu, tiles, tm, tn, sub):
    M, K = x.shape
    _, _, N = wg.shape
    n_n = pl.cdiv(N, tn)
    n_m = M // tm

    def x_map(m, n, te, ns, src):
        return (src[m], 0)

    def w_map(m, n, te, ns, src):
        return (te[m], 0, jnp.where(ns[m] > 0, n, n_n - 1))

    grid_spec = pltpu.PrefetchScalarGridSpec(
        num_scalar_prefetch=3, grid=(n_m, n_n),
        in_specs=[pl.BlockSpec((tm, K), x_map),
                  pl.BlockSpec((None, K, tn), w_map),
                  pl.BlockSpec((None, K, tn), w_map)],
        out_specs=pl.BlockSpec((tm, tn), lambda m, n, te, ns, src: (m, n)))
    return pl.pallas_call(
        functools.partial(_swiglu_kernel, nsub=tm // sub, sub=sub),
        grid_spec=grid_spec, out_shape=jax.ShapeDtypeStruct((M, N), BF16),
        compiler_params=_params(("parallel", "arbitrary")), name="swiglu_up")(*tiles, x, wg, wu)


def _down_kernel(*refs, nsub, sub, n_main, has_scale):
    te_ref, ns_ref, src_ref, xm_ref, xr_ref, wm_ref, wr_ref = refs[:7]
    sc_ref = refs[7] if has_scale else None
    o_ref, acc_ref = refs[-2], refs[-1]
    ns = ns_ref[pl.program_id(0)]
    k = pl.program_id(2)
    for v in range(1, nsub + 1):
        rows = v * sub

        @pl.when((ns == v) & (k == 0))
        def _(rows=rows):
            acc_ref[:rows, :] = jnp.dot(xm_ref[:rows, :], wm_ref[...].astype(BF16), preferred_element_type=F32)

        @pl.when((ns == v) & (k > 0) & (k < n_main))
        def _(rows=rows):
            acc_ref[:rows, :] += jnp.dot(xm_ref[:rows, :], wm_ref[...].astype(BF16), preferred_element_type=F32)

        @pl.when((ns == v) & (k == n_main))
        def _(rows=rows):
            acc = acc_ref[:rows, :] + jnp.dot(xr_ref[:rows, :], wr_ref[...].astype(BF16),
                                              preferred_element_type=F32)
            if has_scale:
                acc = acc * sc_ref[:rows, :]
            o_ref[:rows, :] = acc.astype(o_ref.dtype)
            if rows < nsub * sub:
                o_ref[rows:, :] = jnp.zeros((nsub * sub - rows, o_ref.shape[1]), o_ref.dtype)

    @pl.when((ns == 0) & (k == n_main))
    def _():
        o_ref[...] = jnp.zeros(o_ref.shape, o_ref.dtype)


def _down(x, w3, tiles, tm, tn, tk, tk_rem, sub, out_dtype, row_scale=None):
    M, K = x.shape
    _, _, N = w3.shape
    n_main = (K - tk_rem) // tk
    assert n_main * tk + tk_rem == K and (K - tk_rem) % tk_rem == 0
    rem_idx = (K - tk_rem) // tk_rem
    n_n = N // tn
    n_m = M // tm

    def km(m, k, ns):
        return jnp.where(ns[m] > 0, jnp.minimum(k, n_main - 1), n_main - 1)

    def nn(m, n, ns):
        return jnp.where(ns[m] > 0, n, n_n - 1)

    in_specs = [
        pl.BlockSpec((tm, tk), lambda m, n, k, te, ns, src: (src[m], km(m, k, ns))),
        pl.BlockSpec((tm, tk_rem), lambda m, n, k, te, ns, src: (src[m], rem_idx)),
        pl.BlockSpec((None, tk, tn), lambda m, n, k, te, ns, src: (te[m], km(m, k, ns), nn(m, n, ns))),
        pl.BlockSpec((None, tk_rem, tn), lambda m, n, k, te, ns, src: (te[m], rem_idx, nn(m, n, ns))),
    ]
    args = [*tiles, x, x, w3, w3]
    if row_scale is not None:
        in_specs.append(pl.BlockSpec((tm, 1), lambda m, n, k, te, ns, src: (src[m], 0)))
        args.append(row_scale)
    grid_spec = pltpu.PrefetchScalarGridSpec(
        num_scalar_prefetch=3, grid=(n_m, n_n, n_main + 1), in_specs=in_specs,
        out_specs=pl.BlockSpec((tm, tn), lambda m, n, k, te, ns, src: (m, n)),
        scratch_shapes=[pltpu.VMEM((tm, tn), F32)])
    return pl.pallas_call(
        functools.partial(_down_kernel, nsub=tm // sub, sub=sub, n_main=n_main, has_scale=row_scale is not None),
        grid_spec=grid_spec, out_shape=jax.ShapeDtypeStruct((M, N), out_dtype),
        compiler_params=_params(("parallel", "arbitrary", "arbitrary")), name="down")(*args)


def _rope(x, c, sa, sb):
    return x * c + pltpu.roll(x, LANE - MLA_ROPE // 2, 1) * sa + pltpu.roll(x, MLA_ROPE // 2, 1) * sb


def _qup_kernel(cq_ref, g_ref, w_ref, c_ref, sa_ref, sb_ref, o_ref, xn_ref, *, scale):
    @pl.when(pl.program_id(1) == 0)
    def _():
        xn_ref[...] = _rms(cq_ref[...].astype(F32), g_ref[...]).astype(BF16)

    r = jnp.dot(xn_ref[...], w_ref[...], preferred_element_type=F32)
    o_ref[:, :MLA_NOPE] = (r[:, :MLA_NOPE] * scale).astype(o_ref.dtype)
    hi = _rope(r[:, MLA_NOPE:], c_ref[...], sa_ref[...], sb_ref[...])
    o_ref[:, MLA_NOPE:] = (hi * scale).astype(o_ref.dtype)


def _kvup_kernel(ckv_ref, kpe_ref, g_ref, w_ref, c_ref, sa_ref, sb_ref, k_ref, v_ref, xn_ref):
    @pl.when(pl.program_id(1) == 0)
    def _():
        xn_ref[...] = _rms(ckv_ref[...].astype(F32), g_ref[...]).astype(BF16)

    r = jnp.dot(xn_ref[...], w_ref[...], preferred_element_type=F32)
    k_ref[:, :MLA_NOPE] = r[:, :MLA_NOPE].astype(k_ref.dtype)
    k_ref[:, MLA_NOPE:] = _rope(kpe_ref[...].astype(F32), c_ref[...], sa_ref[...], sb_ref[...]).astype(k_ref.dtype)
    v_ref[...] = r[:, MLA_NOPE:].astype(v_ref.dtype)


def _mla_project(u, B, S, q_g, wq, kv_g, wkv, tabs, tm=512):
    T = B * S
    n_s = S // tm
    H = MLA_HEADS
    tab_spec = pl.BlockSpec((tm, LANE), lambda m, h: (m % n_s, 0))

    def head_spec(width):
        return pl.BlockSpec((None, None, tm, width), lambda m, h: (m // n_s, h, m % n_s, 0))

    q = pl.pallas_call(
        functools.partial(_qup_kernel, scale=MLA_QK ** -0.5),
        grid=(T // tm, H),
        in_specs=[pl.BlockSpec((tm, MLA_Q_LORA), lambda m, h: (m, U_CQ // MLA_Q_LORA)),
                  pl.BlockSpec((1, MLA_Q_LORA), lambda m, h: (0, 0)),
                  pl.BlockSpec((None, MLA_Q_LORA, MLA_HEAD_PAD), lambda m, h: (h, 0, 0)),
                  tab_spec, tab_spec, tab_spec],
        out_specs=head_spec(MLA_HEAD_PAD),
        out_shape=jax.ShapeDtypeStruct((B, H, S, MLA_HEAD_PAD), BF16),
        scratch_shapes=[pltpu.VMEM((tm, MLA_Q_LORA), BF16)],
        compiler_params=_params(("parallel", "arbitrary")), name="mla_q_up")(u, q_g, wq, *tabs)
    k, v = pl.pallas_call(
        _kvup_kernel,
        grid=(T // tm, H),
        in_specs=[pl.BlockSpec((tm, MLA_KV_LORA), lambda m, h: (m, U_CKV // MLA_KV_LORA)),
                  pl.BlockSpec((tm, LANE), lambda m, h: (m, U_KPE // LANE)),
                  pl.BlockSpec((1, MLA_KV_LORA), lambda m, h: (0, 0)),
                  pl.BlockSpec((None, MLA_KV_LORA, MLA_NOPE + MLA_V), lambda m, h: (h, 0, 0)),
                  tab_spec, tab_spec, tab_spec],
        out_specs=[head_spec(MLA_HEAD_PAD), head_spec(MLA_V)],
        out_shape=[jax.ShapeDtypeStruct((B, H, S, MLA_HEAD_PAD), BF16),
                   jax.ShapeDtypeStruct((B, H, S, MLA_V), BF16)],
        scratch_shapes=[pltpu.VMEM((tm, MLA_KV_LORA), BF16)],
        compiler_params=_params(("parallel", "arbitrary")), name="mla_kv_up")(u, u, kv_g, wkv, *tabs)
    return q, k, v


def _flash_kernel(q_ref, k_ref, v_ref, o_ref, *, tq, tk):
    qi = pl.program_id(2)
    q = q_ref[...]

    def step(j, carry, masked):
        m, l, acc = carry
        start = pl.multiple_of(j * tk, tk)
        ks = k_ref[pl.ds(start, tk), :]
        vs = v_ref[pl.ds(start, tk), :]
        s = lax.dot_general(q, ks, (((1,), (1,)), ((), ())), preferred_element_type=F32)
        if masked:
            row = lax.broadcasted_iota(jnp.int32, (tq, tk), 0)
            col = lax.broadcasted_iota(jnp.int32, (tq, tk), 1)
            s = jnp.where(row >= col, s, NEG_INF)
        m_new = jnp.maximum(m, jnp.max(s, axis=-1, keepdims=True))
        alpha = jnp.exp(m - m_new)
        p = jnp.exp(s - m_new)
        l = alpha * l + jnp.sum(p, axis=-1, keepdims=True)
        acc = alpha * acc + jnp.dot(p.astype(BF16), vs, preferred_element_type=F32)
        return m_new, l, acc

    init = (jnp.full((tq, 1), NEG_INF, F32), jnp.zeros((tq, 1), F32), jnp.zeros((tq, MLA_V), F32))
    carry = lax.fori_loop(0, qi, lambda j, c: step(j, c, False), init)
    _, l, acc = step(qi, carry, True)
    o_ref[...] = (acc / l).astype(o_ref.dtype)


def _mla_attention(q, k, v, tq=512):
    B, H, S, _ = q.shape
    return pl.pallas_call(
        functools.partial(_flash_kernel, tq=tq, tk=tq),
        grid=(B, H, S // tq),
        in_specs=[pl.BlockSpec((None, None, tq, MLA_HEAD_PAD), lambda b, h, i: (b, h, i, 0)),
                  pl.BlockSpec((None, None, S, MLA_HEAD_PAD), lambda b, h, i: (b, h, 0, 0)),
                  pl.BlockSpec((None, None, S, MLA_V), lambda b, h, i: (b, h, 0, 0))],
        out_specs=pl.BlockSpec((None, tq, MLA_V), lambda b, h, i: (b, i, h)),
        out_shape=jax.ShapeDtypeStruct((B, S, H * MLA_V), BF16),
        compiler_params=_params(("parallel", "parallel", "arbitrary")), name="mla_attention")(q, k, v)


def _pool_kernel(u_ref, w_ref, sc_ref, gn_ref, o_ref, buf_ref, *, ts):
    s = pl.program_id(1)

    @pl.when(s == 0)
    def _():
        buf_ref[0:POOL_HALO, :] = jnp.zeros((POOL_HALO, GROUP_WIDTH), F32)

    @pl.when(s > 0)
    def _():
        buf_ref[0:POOL_HALO, :] = buf_ref[ts:ts + POOL_HALO, :]

    buf_ref[POOL_HALO:POOL_HALO + ts, :] = u_ref[...].astype(F32)
    pos = s * ts + lax.broadcasted_iota(jnp.int32, (ts, 1), 0)
    ys = []
    ss = jnp.zeros((ts, 1), F32)
    for gi, w in enumerate(POOL_WINDOWS):
        lanes = slice(gi * POOL_GROUP, (gi + 1) * POOL_GROUP)
        cur = buf_ref[POOL_HALO:POOL_HALO + ts, lanes]
        win = cur
        for back in range(1, w):
            win = win + buf_ref[POOL_HALO - back:POOL_HALO - back + ts, lanes]
        count = jnp.minimum(pos + 1, w).astype(F32)
        d = win / count - cur
        y = jnp.dot(d.astype(BF16), w_ref[gi], preferred_element_type=F32) * sc_ref[:, lanes]
        ss = ss + jnp.sum(y * y, axis=-1, keepdims=True)
        ys.append(y)
    r = lax.rsqrt(ss / GROUP_WIDTH + EPS)
    for gi, y in enumerate(ys):
        lanes = slice(gi * POOL_GROUP, (gi + 1) * POOL_GROUP)
        o_ref[:, lanes] = (y * r * gn_ref[:, lanes]).astype(o_ref.dtype)


def _pool(u3, w_pool, pool_scale, gn_g, ts=512):
    B, S, _ = u3.shape
    return pl.pallas_call(
        functools.partial(_pool_kernel, ts=ts),
        grid=(B, S // ts),
        in_specs=[pl.BlockSpec((None, ts, GROUP_WIDTH), lambda b, s: (b, s, U_POOL // GROUP_WIDTH)),
                  pl.BlockSpec((len(POOL_WINDOWS), POOL_GROUP, POOL_GROUP), lambda b, s: (0, 0, 0)),
                  pl.BlockSpec((1, GROUP_WIDTH), lambda b, s: (0, 0)),
                  pl.BlockSpec((1, GROUP_WIDTH), lambda b, s: (0, 0))],
        out_specs=pl.BlockSpec((None, ts, GROUP_WIDTH), lambda b, s: (b, s, 0)),
        out_shape=jax.ShapeDtypeStruct((B, S, GROUP_WIDTH), BF16),
        scratch_shapes=[pltpu.VMEM((POOL_HALO + ts, GROUP_WIDTH), F32)],
        compiler_params=_params(("parallel", "arbitrary")), name="pool_mixer")(u3, w_pool, pool_scale, gn_g)


def _conv_kernel(a_ref, gate_ref, wdw_ref, bdw_ref, lng_ref, lnb_ref, wpw_ref, gn_ref, o_ref,
                 buf_ref, zc_ref, *, ts, rc):
    s = pl.program_id(1)

    @pl.when(s == 0)
    def _():
        buf_ref[0:CONV_HALO, :] = jnp.zeros((CONV_HALO, GROUP_WIDTH), F32)

    @pl.when(s > 0)
    def _():
        buf_ref[0:CONV_HALO, :] = buf_ref[ts:ts + CONV_HALO, :]

    a = a_ref[...].astype(F32)
    gate = gate_ref[...].astype(F32)
    buf_ref[CONV_HALO:CONV_HALO + ts, :] = a * jax.nn.sigmoid(gate)
    first = CONV_HALO - (CONV_WIDTH - 1)

    def lane_block(c, _):
        lanes = pl.ds(pl.multiple_of(c * LANE, LANE), LANE)
        for r0 in range(0, ts, rc):
            acc = jnp.broadcast_to(bdw_ref[:, lanes], (rc, LANE))
            for j in range(CONV_WIDTH):
                acc = acc + buf_ref[first + r0 + j:first + r0 + j + rc, lanes] * wdw_ref[j:j + 1, lanes]
            zc_ref[r0:r0 + rc, lanes] = acc
        return 0

    lax.fori_loop(0, GROUP_WIDTH // LANE, lane_block, 0)
    z = zc_ref[...]
    mu = jnp.mean(z, axis=-1, keepdims=True)
    zc = z - mu
    zn = zc * lax.rsqrt(jnp.mean(zc * zc, axis=-1, keepdims=True) + EPS) * lng_ref[...] + lnb_ref[...]
    act = zn * jax.nn.sigmoid(zn)
    y = jnp.dot(act.astype(BF16), wpw_ref[...], preferred_element_type=F32)
    o_ref[...] = _rms(y, gn_ref[...]).astype(o_ref.dtype)


def _conv(u3, w_dw, b_dw, ln_g, ln_b, w_pw, gn_g, ts=256, rc=64):
    B, S, _ = u3.shape
    vec = pl.BlockSpec((1, GROUP_WIDTH), lambda b, s: (0, 0))
    return pl.pallas_call(
        functools.partial(_conv_kernel, ts=ts, rc=rc),
        grid=(B, S // ts),
        in_specs=[pl.BlockSpec((None, ts, GROUP_WIDTH), lambda b, s: (b, s, U_CONV_A // GROUP_WIDTH)),
                  pl.BlockSpec((None, ts, GROUP_WIDTH), lambda b, s: (b, s, U_CONV_G // GROUP_WIDTH)),
                  pl.BlockSpec((CONV_HALO, GROUP_WIDTH), lambda b, s: (0, 0)),
                  vec, vec, vec,
                  pl.BlockSpec((GROUP_WIDTH, GROUP_WIDTH), lambda b, s: (0, 0)),
                  vec],
        out_specs=pl.BlockSpec((None, ts, GROUP_WIDTH), lambda b, s: (b, s, 0)),
        out_shape=jax.ShapeDtypeStruct((B, S, GROUP_WIDTH), BF16),
        scratch_shapes=[pltpu.VMEM((CONV_HALO + ts, GROUP_WIDTH), F32), pltpu.VMEM((ts, GROUP_WIDTH), F32)],
        compiler_params=_params(("parallel", "arbitrary")), name="conv_mixer")(
            u3, u3, w_dw, b_dw, ln_g, ln_b, w_pw, gn_g)


def _swa_kernel(sink_ref, q_ref, kp_ref, kc_ref, vp_ref, vc_ref, gn_ref, o_ref):
    n = pl.program_id(1)
    W = SWA_WINDOW
    dh = SWA_HEAD_DIM
    R = SWA_Q_HEADS // SWA_KV_HEADS
    q = q_ref[...] * (dh ** -0.5)
    k2 = jnp.concatenate([kp_ref[...], kc_ref[...]], axis=0)
    v2 = jnp.concatenate([vp_ref[...], vc_ref[...]], axis=0)
    qi = lax.broadcasted_iota(jnp.int32, (W, 2 * W), 0)
    kj = lax.broadcasted_iota(jnp.int32, (W, 2 * W), 1)
    rel = qi + W - kj
    valid = (rel >= 0) & (rel < W) & (n * W + kj - W >= 0)
    outs = []
    for g in range(SWA_KV_HEADS):
        kg = k2[:, g * dh:(g + 1) * dh]
        vg = v2[:, g * dh:(g + 1) * dh]
        for r in range(R):
            h = g * R + r
            s = lax.dot_general(q[:, h * dh:(h + 1) * dh], kg, (((1,), (1,)), ((), ())),
                                preferred_element_type=F32)
            s = jnp.where(valid, s, NEG_INF)
            sink = sink_ref[h]
            m = jnp.maximum(jnp.max(s, axis=-1, keepdims=True), sink)
            e = jnp.exp(s - m)
            denom = jnp.sum(e, axis=-1, keepdims=True) + jnp.exp(sink - m)
            p = e / denom
            outs.append(jnp.dot(p.astype(BF16), vg, preferred_element_type=F32))
    y = jnp.concatenate(outs, axis=-1)
    o_ref[...] = _rms(y, gn_ref[...]).astype(o_ref.dtype)


def _swa(u3, sinks, gn_g):
    B, S, _ = u3.shape
    W = SWA_WINDOW
    kcol, vcol = U_SWA_K // LANE, U_SWA_V // LANE

    def prev(col):
        return pl.BlockSpec((None, W, LANE), lambda b, n, sk: (b, jnp.maximum(n - 1, 0), col))

    def cur(col):
        return pl.BlockSpec((None, W, LANE), lambda b, n, sk: (b, n, col))

    grid_spec = pltpu.PrefetchScalarGridSpec(
        num_scalar_prefetch=1, grid=(B, S // W),
        in_specs=[pl.BlockSpec((None, W, GROUP_WIDTH), lambda b, n, sk: (b, n, U_SWA_Q // GROUP_WIDTH)),
                  prev(kcol), cur(kcol), prev(vcol), cur(vcol),
                  pl.BlockSpec((1, GROUP_WIDTH), lambda b, n, sk: (0, 0))],
        out_specs=pl.BlockSpec((None, W, GROUP_WIDTH), lambda b, n, sk: (b, n, 0)))
    return pl.pallas_call(
        _swa_kernel, grid_spec=grid_spec,
        out_shape=jax.ShapeDtypeStruct((B, S, GROUP_WIDTH), BF16),
        compiler_params=_params(("parallel", "arbitrary")), name="swa_mixer")(
            sinks, u3, u3, u3, u3, u3, gn_g)


def _router_kernel(x_ref, g_ref, w_ref, o_ref, cnt_ref, carry_ref, *, tm):
    @pl.when(pl.program_id(0) == 0)
    def _():
        carry_ref[...] = jnp.zeros_like(carry_ref)

    h = _rms(x_ref[...], g_ref[...])
    logits = jnp.dot(h.astype(BF16), w_ref[...], preferred_element_type=F32)
    lane = lax.broadcasted_iota(jnp.int32, (tm, LANE), 1).astype(F32)
    logits = jnp.where(lane < N_EXPERTS, logits, -jnp.inf)
    m1 = jnp.max(logits, axis=-1, keepdims=True)
    i1 = jnp.min(jnp.where(logits == m1, lane, float(LANE)), axis=-1, keepdims=True)
    rest = jnp.where(lane == i1, -jnp.inf, logits)
    m2 = jnp.max(rest, axis=-1, keepdims=True)
    i2 = jnp.min(jnp.where(rest == m2, lane, float(LANE)), axis=-1, keepdims=True)
    e2 = jnp.exp(m2 - m1)
    w1 = 1.0 / (1.0 + e2)
    w2 = e2 / (1.0 + e2)
    oh1 = (lane == i1).astype(F32)
    oh2 = (lane == i2).astype(F32)
    cnt = oh1 + oh2
    row = lax.broadcasted_iota(jnp.int32, (tm, tm), 0)
    col = lax.broadcasted_iota(jnp.int32, (tm, tm), 1)
    before = (row > col).astype(BF16)
    pre = jnp.dot(before, cnt.astype(BF16), preferred_element_type=F32) + carry_ref[0:1, :]
    r1 = jnp.sum(oh1 * pre, axis=-1, keepdims=True)
    r2 = jnp.sum(oh2 * pre, axis=-1, keepdims=True)
    carry_ref[0:1, :] = carry_ref[0:1, :] + jnp.sum(cnt, axis=0, keepdims=True)
    cols = (i1, i2, r1, r2, w1, w2)
    out = jnp.zeros((tm, LANE), F32)
    for ci, val in enumerate(cols):
        out = jnp.where(lane == ci, val, out)
    o_ref[...] = out
    cnt_ref[...] = jnp.broadcast_to(carry_ref[0:1, :], cnt_ref.shape)


def _router(x, g, w_router_pad, tm=256):
    T, D = x.shape
    return pl.pallas_call(
        functools.partial(_router_kernel, tm=tm),
        grid=(T // tm,),
        in_specs=[pl.BlockSpec((tm, D), lambda i: (i, 0)),
                  pl.BlockSpec((1, D), lambda i: (0, 0)),
                  pl.BlockSpec((D, LANE), lambda i: (0, 0))],
        out_specs=[pl.BlockSpec((tm, LANE), lambda i: (i, 0)),
                   pl.BlockSpec((8, LANE), lambda i: (0, 0))],
        out_shape=[jax.ShapeDtypeStruct((T, LANE), F32), jax.ShapeDtypeStruct((8, LANE), F32)],
        scratch_shapes=[pltpu.VMEM((8, LANE), F32)],
        compiler_params=_params(("arbitrary",)), name="router")(x, g, w_router_pad)


def _row_copy(src_hbm, row, dst, slot, sem):
    return pltpu.make_async_copy(src_hbm.at[pl.ds(row, 1), :], dst.at[pl.ds(slot, 1), :], sem)


def _dispatch_kernel(tok_ref, ns_ref, x_hbm, g_ref, o_ref, buf_ref, sem, *, rows, per_tile):
    i = pl.program_id(0)
    live = i % per_tile < ns_ref[i // per_tile]

    @pl.when(jnp.logical_not(live))
    def _():
        o_ref[...] = jnp.zeros(o_ref.shape, o_ref.dtype)

    @pl.when(live)
    def _():
        def start(r, _):
            _row_copy(x_hbm, tok_ref[i * rows + r], buf_ref, r, sem).start()
            return 0

        def wait(r, _):
            _row_copy(x_hbm, tok_ref[i * rows + r], buf_ref, r, sem).wait()
            return 0

        lax.fori_loop(0, rows, start, 0)
        lax.fori_loop(0, rows, wait, 0)
        o_ref[...] = _rms(buf_ref[...], g_ref[...]).astype(o_ref.dtype)


def _dispatch(x, g, row_token, tile_nsub, n_rows):
    T, D = x.shape
    rows = MOE_SUB
    grid_spec = pltpu.PrefetchScalarGridSpec(
        num_scalar_prefetch=2, grid=(n_rows // rows,),
        in_specs=[pl.BlockSpec(memory_space=pl.ANY),
                  pl.BlockSpec((1, D), lambda i, tok, ns: (0, 0))],
        out_specs=pl.BlockSpec((rows, D), lambda i, tok, ns: (i, 0)),
        scratch_shapes=[pltpu.VMEM((rows, D), F32), pltpu.SemaphoreType.DMA(())])
    return pl.pallas_call(
        functools.partial(_dispatch_kernel, rows=rows, per_tile=MOE_TILE // MOE_SUB),
        grid_spec=grid_spec, out_shape=jax.ShapeDtypeStruct((n_rows, D), BF16),
        compiler_params=_params(("arbitrary",)), name="moe_dispatch")(row_token, tile_nsub, x, g)


def _combine_kernel(p0_ref, p1_ref, x_ref, eo_hbm, g_ref, o_ref, a_ref, b_ref, sem, *, rows):
    i = pl.program_id(0)

    def start(r, _):
        _row_copy(eo_hbm, p0_ref[i * rows + r], a_ref, r, sem).start()
        _row_copy(eo_hbm, p1_ref[i * rows + r], b_ref, r, sem).start()
        return 0

    def wait(r, _):
        _row_copy(eo_hbm, p0_ref[i * rows + r], a_ref, r, sem).wait()
        _row_copy(eo_hbm, p1_ref[i * rows + r], b_ref, r, sem).wait()
        return 0

    lax.fori_loop(0, rows, start, 0)
    lax.fori_loop(0, rows, wait, 0)
    y = x_ref[...] + (a_ref[...] + b_ref[...])
    o_ref[...] = _rms(y, g_ref[...]).astype(o_ref.dtype)


def _combine(x, eo, pos0, pos1, g, rows=128):
    T, D = x.shape
    grid_spec = pltpu.PrefetchScalarGridSpec(
        num_scalar_prefetch=2, grid=(T // rows,),
        in_specs=[pl.BlockSpec((rows, D), lambda i, p0, p1: (i, 0)),
                  pl.BlockSpec(memory_space=pl.ANY),
                  pl.BlockSpec((1, D), lambda i, p0, p1: (0, 0))],
        out_specs=pl.BlockSpec((rows, D), lambda i, p0, p1: (i, 0)),
        scratch_shapes=[pltpu.VMEM((rows, D), F32), pltpu.VMEM((rows, D), F32), pltpu.SemaphoreType.DMA(())])
    return pl.pallas_call(
        functools.partial(_combine_kernel, rows=rows),
        grid_spec=grid_spec, out_shape=jax.ShapeDtypeStruct((T, D), F32),
        compiler_params=_params(("arbitrary",)), name="moe_combine")(pos0, pos1, x, eo, g)


def _pack_w_in(w_in):
    L, D, _ = w_in.shape
    off_pool, off_conv, off_swa = 1600, 2624, 4672
    z = jnp.zeros((L, D, 64), w_in.dtype)
    parts = [
        w_in[..., off_pool:off_pool + 1024],
        w_in[..., off_conv:off_conv + 2048],
        w_in[..., off_swa:off_swa + 1024],
        w_in[..., 0:1536],
        w_in[..., 1536:1600], z,
        w_in[..., off_swa + 1024:off_swa + 1280],
        z, z,
    ]
    return jnp.concatenate(parts, axis=-1).astype(BF16)


def _rope_tables(S):
    inv = 1.0 / (ROPE_THETA ** (jnp.arange(0, MLA_ROPE, 2, dtype=F32) / MLA_ROPE))
    ang = jnp.arange(S, dtype=F32)[:, None] * inv[None, :]
    cos, sin = jnp.cos(ang), jnp.sin(ang)
    z32 = jnp.zeros_like(cos)
    z64 = jnp.zeros((S, 64), F32)
    c = jnp.concatenate([cos, cos, z64], axis=-1)
    sa = jnp.concatenate([-sin, z32, z64], axis=-1)
    sb = jnp.concatenate([z32, sin, z64], axis=-1)
    return c, sa, sb


def _moe_plan(route, counts, T, n_tiles):
    E = N_EXPERTS
    expert = route[:, 0:2].astype(jnp.int32)
    rank = route[:, 2:4].astype(jnp.int32)
    gate = route[:, 4:6]
    counts = counts.astype(jnp.int32)
    tiles_per = (counts + MOE_TILE - 1) // MOE_TILE
    tile_end = jnp.cumsum(tiles_per)
    tile_start = tile_end - tiles_per
    used = tile_end[E - 1]
    pos = tile_start[expert] * MOE_TILE + rank
    n_rows = n_tiles * MOE_TILE
    flat = pos.reshape(-1)
    token = jnp.repeat(jnp.arange(T, dtype=jnp.int32), 2)
    row_token = jnp.zeros((n_rows,), jnp.int32).at[flat].set(token)
    row_gate = jnp.zeros((n_rows,), F32).at[flat].set(gate.reshape(-1))
    t = jnp.arange(n_tiles, dtype=jnp.int32)
    te = jnp.minimum(jnp.sum((t[:, None] >= tile_end[None, :]).astype(jnp.int32), axis=1), E - 1)
    last = jnp.maximum(used - 1, 0)
    te = jnp.where(t < used, te, te[last])
    live_rows = jnp.clip(counts[te] - (t - tile_start[te]) * MOE_TILE, 0, MOE_TILE)
    nsub = jnp.where(t < used, (live_rows + MOE_SUB - 1) // MOE_SUB, 0).astype(jnp.int32)
    src = jnp.minimum(t, last)
    return pos[:, 0], pos[:, 1], row_token, row_gate.reshape(n_rows, 1), (te.astype(jnp.int32), nsub, src)


def kernel(x, attn_norm_g, w_in, mla_q_norm_g, mla_w_q_up, mla_kv_norm_g, mla_w_kv_up, pool_w, pool_scale,
           conv_w_dw, conv_b_dw, conv_ln_g, conv_ln_b, conv_w_pw, swa_sinks, group_out_g, w_out, ffn_norm_g,
           dense_w_gate, dense_w_up, dense_w_down, moe_w_router, moe_w_gate, moe_w_up, moe_w_down, final_norm_g):
    B, S, D = x.shape
    T = B * S
    L = w_in.shape[0]
    H = MLA_HEADS
    x = x.reshape(T, D)

    w_in_p = _pack_w_in(w_in)
    wq = mla_w_q_up.reshape(L, MLA_Q_LORA, H, MLA_QK).transpose(0, 2, 1, 3)
    wq = jnp.pad(wq, ((0, 0), (0, 0), (0, 0), (0, MLA_HEAD_PAD - MLA_QK))).astype(BF16)
    wkv = mla_w_kv_up.reshape(L, MLA_KV_LORA, H, MLA_NOPE + MLA_V).transpose(0, 2, 1, 3).astype(BF16)
    tabs = _rope_tables(S)
    pool_w_b = pool_w.astype(BF16)
    conv_w_pw_b = conv_w_pw.astype(BF16)
    conv_w_dw_p = jnp.pad(conv_w_dw, ((0, 0), (0, CONV_HALO - CONV_WIDTH), (0, 0)))
    gn = group_out_g.reshape(L, 4, 1, GROUP_WIDTH)
    row = lambda v: v.reshape(1, -1).astype(F32)

    assert L == 2, "layer 0 dense FFN, layer 1 (last) expert FFN"
    n_dense_tiles = T // MOE_TILE
    dense_tiles = (jnp.zeros((n_dense_tiles,), jnp.int32), jnp.ones((n_dense_tiles,), jnp.int32),
                   jnp.arange(n_dense_tiles, dtype=jnp.int32))

    delta = None
    out = None
    for l in range(L):
        if delta is None:
            h = _norm(x, attn_norm_g[l])
        else:
            x, h = _norm(x, attn_norm_g[l], delta=delta, write_sum=True)
        u = _mm([h], w_in_p, l, tm=1024, tn=1024, out_dtype=BF16)
        u3 = u.reshape(B, S, U_WIDTH)
        q, k, v = _mla_project(u, B, S, row(mla_q_norm_g[l]), wq[l], row(mla_kv_norm_g[l]), wkv[l], tabs)
        y_a = _mla_attention(q, k, v).reshape(T, GROUP_WIDTH)
        y_a = _norm(y_a, gn[l, 0], out_dtype=BF16, tm=1024)
        y_b = _pool(u3, pool_w_b[l], row(pool_scale[l]), gn[l, 1]).reshape(T, GROUP_WIDTH)
        y_c = _conv(u3, conv_w_dw_p[l], row(conv_b_dw[l]), row(conv_ln_g[l]), row(conv_ln_b[l]),
                    conv_w_pw_b[l], gn[l, 2]).reshape(T, GROUP_WIDTH)
        y_d = _swa(u3, swa_sinks[l].astype(F32), gn[l, 3]).reshape(T, GROUP_WIDTH)
        x = _mm([y_a, y_b, y_c, y_d], w_out, l, tm=1024, tn=512, out_dtype=F32, res=x)
        i = l // 2
        if l % 2 == 0:
            h = _norm(x, ffn_norm_g[l])
            act = _swiglu(h, dense_w_gate, dense_w_up, dense_tiles, tm=MOE_TILE, tn=256, sub=MOE_TILE)
            delta = _down(act, dense_w_down, dense_tiles, tm=MOE_TILE, tn=1024, tk=1536, tk_rem=256,
                          sub=MOE_TILE, out_dtype=BF16)
        else:
            g = row(ffn_norm_g[l])
            w_r = jnp.pad(moe_w_router[i], ((0, 0), (0, LANE - N_EXPERTS))).astype(BF16)
            route, counts = _router(x, g, w_r)
            n_tiles = (2 * T) // MOE_TILE + N_EXPERTS
            pos0, pos1, row_token, row_gate, tiles = _moe_plan(route, counts[0, :N_EXPERTS], T, n_tiles)
            xs = _dispatch(x, g, row_token, tiles[1], n_tiles * MOE_TILE)
            act = _swiglu(xs, moe_w_gate[i], moe_w_up[i], tiles, tm=MOE_TILE, tn=256, sub=MOE_SUB)
            eo = _down(act, moe_w_down[i], tiles, tm=MOE_TILE, tn=512, tk=1792, tk_rem=128, sub=MOE_SUB,
                       out_dtype=F32, row_scale=row_gate)
            out = _combine(x, eo, pos0, pos1, row(final_norm_g))
    return out.reshape(B, S, D)
```

```python
import functools

import jax
import jax.numpy as jnp
from jax import lax
from jax.experimental import pallas as pl
from jax.experimental.pallas import tpu as pltpu

F32 = jnp.float32
BF16 = jnp.bfloat16
EPS = 1e-6
NEG_INF = -1e30

GROUP_WIDTH = 1024
MLA_NOPE = 128
MLA_ROPE = 64
MLA_V = 128
MLA_HEADS = 8
MLA_QK = MLA_NOPE + MLA_ROPE
MLA_Q_LORA = 1024
MLA_KV_LORA = 512
MLA_HEAD_PAD = 256
ROPE_THETA = 10000.0
POOL_WINDOWS = (2, 4, 8, 16)
POOL_GROUP = 256
POOL_HALO = 16
CONV_WIDTH = 31
CONV_HALO = 32
SWA_HEAD_DIM = 64
SWA_Q_HEADS = 16
SWA_KV_HEADS = 2
SWA_WINDOW = 128
N_EXPERTS = 8

U_POOL = 0
U_CONV_A = 1024
U_CONV_G = 2048
U_SWA_Q = 3072
U_CQ = 4096
U_CKV = 5120
U_KPE = 5632
U_SWA_K = 5760
U_SWA_V = 5888
U_WIDTH = 6144

LANE = 128
MOE_TILE = 1024
MOE_SUB = 256
VMEM_LIMIT = 56 * 1024 * 1024


def _params(sem, vmem=VMEM_LIMIT):
    return pltpu.CompilerParams(dimension_semantics=sem, vmem_limit_bytes=vmem)


def _rms(x, g):
    return x * lax.rsqrt(jnp.mean(x * x, axis=-1, keepdims=True) + EPS) * g


def _norm_kernel(*refs, has_delta, write_sum, out_dtype):
    it = iter(refs)
    x_ref = next(it)
    d_ref = next(it) if has_delta else None
    g_ref = next(it)
    s_ref = next(it) if write_sum else None
    o_ref = next(it)
    x = x_ref[...]
    if has_delta:
        x = x + d_ref[...].astype(F32)
    if write_sum:
        s_ref[...] = x
    o_ref[...] = _rms(x, g_ref[...]).astype(out_dtype)


def _norm(x, g, delta=None, write_sum=False, out_dtype=BF16, tm=256):
    T, D = x.shape
    row = pl.BlockSpec((tm, D), lambda i: (i, 0))
    in_specs = [row] + ([row] if delta is not None else []) + [pl.BlockSpec((1, D), lambda i: (0, 0))]
    args = [x] + ([delta] if delta is not None else []) + [g.reshape(1, D).astype(F32)]
    out_shape = [jax.ShapeDtypeStruct((T, D), out_dtype)]
    out_specs = [row]
    if write_sum:
        out_shape = [jax.ShapeDtypeStruct((T, D), F32)] + out_shape
        out_specs = [row] + out_specs
    res = pl.pallas_call(
        functools.partial(_norm_kernel, has_delta=delta is not None, write_sum=write_sum, out_dtype=out_dtype),
        grid=(T // tm,), in_specs=in_specs, out_specs=out_specs, out_shape=out_shape,
        compiler_params=_params(("parallel",)), name="norm")(*args)
    return res if write_sum else res[0]


def _mm_kernel(*refs, nx, has_res):
    x_refs = refs[:nx]
    w_ref = refs[nx]
    res_ref = refs[nx + 1] if has_res else None
    o_ref = refs[-1]
    acc = None
    off = 0
    for xr in x_refs:
        kx = xr.shape[1]
        p = jnp.dot(xr[...], w_ref[off:off + kx, :].astype(BF16), preferred_element_type=F32)
        acc = p if acc is None else acc + p
        off += kx
    if has_res:
        acc = acc + res_ref[...]
    o_ref[...] = acc.astype(o_ref.dtype)


def _mm(xs, w3, g, tm, tn, out_dtype, res=None):
    M = xs[0].shape[0]
    _, K, N = w3.shape
    assert sum(x.shape[1] for x in xs) == K
    in_specs = [pl.BlockSpec((tm, x.shape[1]), lambda m, n: (m, 0)) for x in xs]
    in_specs.append(pl.BlockSpec((None, K, tn), lambda m, n: (g, 0, n)))
    args = list(xs) + [w3]
    if res is not None:
        in_specs.append(pl.BlockSpec((tm, tn), lambda m, n: (m, n)))
        args.append(res)
    return pl.pallas_call(
        functools.partial(_mm_kernel, nx=len(xs), has_res=res is not None),
        grid=(M // tm, N // tn), in_specs=in_specs,
        out_specs=pl.BlockSpec((tm, tn), lambda m, n: (m, n)),
        out_shape=jax.ShapeDtypeStruct((M, N), out_dtype),
        compiler_params=_params(("parallel", "arbitrary")), name="mm")(*args)


def _swiglu_kernel(te_ref, ns_ref, src_ref, x_ref, wg_ref, wu_ref, o_ref, *, nsub, sub):
    ns = ns_ref[pl.program_id(0)]
    for v in range(nsub + 1):
        @pl.when(ns == v)
        def _(v=v):
            rows = v * sub
            if v > 0:
                x = x_ref[:rows, :]
                g = jnp.dot(x, wg_ref[...].astype(BF16), preferred_element_type=F32)
                u = jnp.dot(x, wu_ref[...].astype(BF16), preferred_element_type=F32)
                o_ref[:rows, :] = (g * jax.nn.sigmoid(g) * u).astype(o_ref.dtype)
            if v < nsub:
                o_ref[rows:, :] = jnp.zeros((nsub * sub - rows, o_ref.shape[1]), o_ref.dtype)


def _swiglu(x, wg, wu, tiles, tm, tn, sub):
    M, K = x.shape
    _, _, N = wg.shape
    n_n = pl.cdiv(N, tn)
    n_m = M // tm

    def x_map(m, n, te, ns, src):
        return (src[m], 0)

    def w_map(m, n, te, ns, src):
        return (te[m], 0, jnp.where(ns[m] > 0, n, n_n - 1))

    grid_spec = pltpu.PrefetchScalarGridSpec(
        num_scalar_prefetch=3, grid=(n_m, n_n),
        in_specs=[pl.BlockSpec((tm, K), x_map),
                  pl.BlockSpec((None, K, tn), w_map),
                  pl.BlockSpec((None, K, tn), w_map)],
        out_specs=pl.BlockSpec((tm, tn), lambda m, n, te, ns, src: (m, n)))
    return pl.pallas_call(
        functools.partial(_swiglu_kernel, nsub=tm // sub, sub=sub),
        grid_spec=grid_spec, out_shape=jax.ShapeDtypeStruct((M, N), BF16),
        compiler_params=_params(("parallel", "arbitrary")), name="swiglu_up")(*tiles, x, wg, wu)


def _down_dense_kernel(xm_ref, xr_ref, wm_ref, wr_ref, o_ref, acc_ref, *, n_main):
    k = pl.program_id(2)

    @pl.when(k == 0)
    def _():
        acc_ref[...] = jnp.dot(xm_ref[...], wm_ref[...].astype(BF16), preferred_element_type=F32)

    @pl.when((k > 0) & (k < n_main))
    def _():
        acc_ref[...] += jnp.dot(xm_ref[...], wm_ref[...].astype(BF16), preferred_element_type=F32)

    @pl.when(k == n_main)
    def _():
        o_ref[...] = (acc_ref[...] + jnp.dot(xr_ref[...], wr_ref[...].astype(BF16),
                                             preferred_element_type=F32)).astype(o_ref.dtype)


def _down_dense(x, w3, g, tm, tn, tk, tk_rem, out_dtype):
    M, K = x.shape
    _, _, N = w3.shape
    n_main = (K - tk_rem) // tk
    assert n_main * tk + tk_rem == K and (K - tk_rem) % tk_rem == 0
    rem_idx = (K - tk_rem) // tk_rem

    def km(k):
        return jnp.minimum(k, n_main - 1)

    return pl.pallas_call(
        functools.partial(_down_dense_kernel, n_main=n_main),
        grid=(M // tm, N // tn, n_main + 1),
        in_specs=[pl.BlockSpec((tm, tk), lambda m, n, k: (m, km(k))),
                  pl.BlockSpec((tm, tk_rem), lambda m, n, k: (m, rem_idx)),
                  pl.BlockSpec((None, tk, tn), lambda m, n, k: (g, km(k), n)),
                  pl.BlockSpec((None, tk_rem, tn), lambda m, n, k: (g, rem_idx, n))],
        out_specs=pl.BlockSpec((tm, tn), lambda m, n, k: (m, n)),
        out_shape=jax.ShapeDtypeStruct((M, N), out_dtype),
        scratch_shapes=[pltpu.VMEM((tm, tn), F32)],
        compiler_params=_params(("parallel", "arbitrary", "arbitrary")), name="down_dense")(x, x, w3, w3)


def _down_grouped_kernel(te_ref, ns_ref, src_ref, x_ref, w_ref, o_ref, *, nsub, sub):
    ns = ns_ref[pl.program_id(0)]
    for v in range(nsub + 1):
        @pl.when(ns == v)
        def _(v=v):
            rows = v * sub
            if v > 0:
                o_ref[:rows, :] = jnp.dot(x_ref[:rows, :], w_ref[...].astype(BF16),
                                          preferred_element_type=F32).astype(o_ref.dtype)
            if v < nsub:
                o_ref[rows:, :] = jnp.zeros((nsub * sub - rows, o_ref.shape[1]), o_ref.dtype)


def _down_grouped(x, w3, tiles, tm, tn, sub, out_dtype):
    M, K = x.shape
    _, _, N = w3.shape
    n_n = N // tn

    def w_map(m, n, te, ns, src):
        return (te[m], 0, jnp.where(ns[m] > 0, n, n_n - 1))

    grid_spec = pltpu.PrefetchScalarGridSpec(
        num_scalar_prefetch=3, grid=(M // tm, n_n),
        in_specs=[pl.BlockSpec((tm, K), lambda m, n, te, ns, src: (src[m], 0)),
                  pl.BlockSpec((None, K, tn), w_map)],
        out_specs=pl.BlockSpec((tm, tn), lambda m, n, te, ns, src: (m, n)))
    return pl.pallas_call(
        functools.partial(_down_grouped_kernel, nsub=tm // sub, sub=sub),
        grid_spec=grid_spec, out_shape=jax.ShapeDtypeStruct((M, N), out_dtype),
        compiler_params=_params(("parallel", "arbitrary")), name="down_grouped")(*tiles, x, w3)


def _rope(x, c, sa, sb):
    return x * c + pltpu.roll(x, LANE - MLA_ROPE // 2, 1) * sa + pltpu.roll(x, MLA_ROPE // 2, 1) * sb


def _mla_project_kernel(cq_ref, ckv_ref, kpe_ref, qg_ref, kvg_ref, wq_ref, wkv_ref, c_ref, sa_ref, sb_ref,
                        q_ref, k_ref, v_ref, *, scale):
    c, sa, sb = c_ref[...], sa_ref[...], sb_ref[...]
    xq = _rms(cq_ref[...].astype(F32), qg_ref[...]).astype(BF16)
    rq = jnp.dot(xq, wq_ref[...], preferred_element_type=F32)
    xkv = _rms(ckv_ref[...].astype(F32), kvg_ref[...]).astype(BF16)
    rkv = jnp.dot(xkv, wkv_ref[...], preferred_element_type=F32)
    kpe = _rope(kpe_ref[...].astype(F32), c, sa, sb).astype(k_ref.dtype)
    for h in range(MLA_HEADS):
        lo = h * MLA_HEAD_PAD
        mid = lo + MLA_NOPE
        hi = lo + MLA_HEAD_PAD
        q_ref[h, :, :MLA_NOPE] = (rq[:, lo:mid] * scale).astype(q_ref.dtype)
        q_ref[h, :, MLA_NOPE:] = (_rope(rq[:, mid:hi], c, sa, sb) * scale).astype(q_ref.dtype)
        k_ref[h, :, :MLA_NOPE] = rkv[:, lo:mid].astype(k_ref.dtype)
        k_ref[h, :, MLA_NOPE:] = kpe
        v_ref[h, :, :] = rkv[:, mid:hi].astype(v_ref.dtype)


def _mla_project(u, B, S, q_g, wq, kv_g, wkv, tabs, tm=512):
    T = B * S
    n_s = S // tm
    H = MLA_HEADS
    tab_spec = pl.BlockSpec((tm, LANE), lambda m: (m % n_s, 0))

    def const(shape):
        return pl.BlockSpec(shape, lambda m: (0,) * len(shape))

    def head_spec(width):
        return pl.BlockSpec((None, H, tm, width), lambda m: (m // n_s, 0, m % n_s, 0))

    return pl.pallas_call(
        functools.partial(_mla_project_kernel, scale=MLA_QK ** -0.5),
        grid=(T // tm,),
        in_specs=[pl.BlockSpec((tm, MLA_Q_LORA), lambda m: (m, U_CQ // MLA_Q_LORA)),
                  pl.BlockSpec((tm, MLA_KV_LORA), lambda m: (m, U_CKV // MLA_KV_LORA)),
                  pl.BlockSpec((tm, LANE), lambda m: (m, U_KPE // LANE)),
                  const((1, MLA_Q_LORA)), const((1, MLA_KV_LORA)),
                  const((MLA_Q_LORA, H * MLA_HEAD_PAD)), const((MLA_KV_LORA, H * MLA_HEAD_PAD)),
                  tab_spec, tab_spec, tab_spec],
        out_specs=[head_spec(MLA_HEAD_PAD), head_spec(MLA_HEAD_PAD), head_spec(MLA_V)],
        out_shape=[jax.ShapeDtypeStruct((B, H, S, MLA_HEAD_PAD), BF16),
                   jax.ShapeDtypeStruct((B, H, S, MLA_HEAD_PAD), BF16),
                   jax.ShapeDtypeStruct((B, H, S, MLA_V), BF16)],
        compiler_params=_params(("parallel",)), name="mla_project")(u, u, u, q_g, kv_g, wq, wkv, *tabs)


def _flash_kernel(q_ref, k_ref, v_ref, o_ref, *, tq, tk, hp):
    qi = pl.program_id(2)

    nk = tq // tk

    def step(j, carry, diag):
        start = pl.multiple_of(j * tk, tk)
        out = []
        for hh in range(hp):
            m, l, acc = carry[hh]
            s = lax.dot_general(q_ref[hh], k_ref[hh, pl.ds(start, tk), :], (((1,), (1,)), ((), ())),
                                preferred_element_type=F32)
            if diag is not None:
                row = lax.broadcasted_iota(jnp.int32, (tq, tk), 0)
                col = lax.broadcasted_iota(jnp.int32, (tq, tk), 1) + diag * tk
                s = jnp.where(row >= col, s, NEG_INF)
            m_new = jnp.maximum(m, jnp.max(s, axis=-1, keepdims=True))
            alpha = jnp.exp(m - m_new)
            p = jnp.exp(s - m_new)
            l = alpha * l + jnp.sum(p, axis=-1, keepdims=True)
            acc = alpha * acc + jnp.dot(p.astype(BF16), v_ref[hh, pl.ds(start, tk), :],
                                        preferred_element_type=F32)
            out.append((m_new, l, acc))
        return tuple(out)

    carry = tuple((jnp.full((tq, 1), NEG_INF, F32), jnp.zeros((tq, 1), F32), jnp.zeros((tq, MLA_V), F32))
                  for _ in range(hp))
    def full_tile(t, c):
        for d in range(nk):
            c = step(t * nk + d, c, None)
        return c

    carry = lax.fori_loop(0, qi, full_tile, carry)
    for d in range(nk):
        carry = step(qi * nk + d, carry, d)
    for hh in range(hp):
        _, l, acc = carry[hh]
        o_ref[:, hh * MLA_V:(hh + 1) * MLA_V] = (acc / l).astype(o_ref.dtype)


def _mla_attention(q, k, v, tq=512, tk=512, hp=2):
    B, H, S, _ = q.shape
    return pl.pallas_call(
        functools.partial(_flash_kernel, tq=tq, tk=tk, hp=hp),
        grid=(B, H // hp, S // tq),
        in_specs=[pl.BlockSpec((None, hp, tq, MLA_HEAD_PAD), lambda b, h, i: (b, h, i, 0)),
                  pl.BlockSpec((None, hp, S, MLA_HEAD_PAD), lambda b, h, i: (b, h, 0, 0)),
                  pl.BlockSpec((None, hp, S, MLA_V), lambda b, h, i: (b, h, 0, 0))],
        out_specs=pl.BlockSpec((None, tq, hp * MLA_V), lambda b, h, i: (b, i, h)),
        out_shape=jax.ShapeDtypeStruct((B, S, H * MLA_V), BF16),
        compiler_params=_params(("parallel", "parallel", "arbitrary")), name="mla_attention")(q, k, v)


def _pool_kernel(u_ref, w_ref, sc_ref, gn_ref, o_ref, buf_ref, *, ts):
    s = pl.program_id(1)

    @pl.when(s == 0)
    def _():
        buf_ref[0:POOL_HALO, :] = jnp.zeros((POOL_HALO, GROUP_WIDTH), F32)

    @pl.when(s > 0)
    def _():
        buf_ref[0:POOL_HALO, :] = buf_ref[ts:ts + POOL_HALO, :]

    buf_ref[POOL_HALO:POOL_HALO + ts, :] = u_ref[...].astype(F32)
    pos = s * ts + lax.broadcasted_iota(jnp.int32, (ts, 1), 0)
    ys = []
    ss = jnp.zeros((ts, 1), F32)
    for gi, w in enumerate(POOL_WINDOWS):
        lanes = slice(gi * POOL_GROUP, (gi + 1) * POOL_GROUP)
        cur = buf_ref[POOL_HALO:POOL_HALO + ts, lanes]
        win = cur
        for back in range(1, w):
            win = win + buf_ref[POOL_HALO - back:POOL_HALO - back + ts, lanes]
        count = jnp.minimum(pos + 1, w).astype(F32)
        d = win / count - cur
        y = jnp.dot(d.astype(BF16), w_ref[gi], preferred_element_type=F32) * sc_ref[:, lanes]
        ss = ss + jnp.sum(y * y, axis=-1, keepdims=True)
        ys.append(y)
    r = lax.rsqrt(ss / GROUP_WIDTH + EPS)
    for gi, y in enumerate(ys):
        lanes = slice(gi * POOL_GROUP, (gi + 1) * POOL_GROUP)
        o_ref[:, lanes] = (y * r * gn_ref[:, lanes]).astype(o_ref.dtype)


def _pool(u3, w_pool, pool_scale, gn_g, ts=512):
    B, S, _ = u3.shape
    return pl.pallas_call(
        functools.partial(_pool_kernel, ts=ts),
        grid=(B, S // ts),
        in_specs=[pl.BlockSpec((None, ts, GROUP_WIDTH), lambda b, s: (b, s, U_POOL // GROUP_WIDTH)),
                  pl.BlockSpec((len(POOL_WINDOWS), POOL_GROUP, POOL_GROUP), lambda b, s: (0, 0, 0)),
                  pl.BlockSpec((1, GROUP_WIDTH), lambda b, s: (0, 0)),
                  pl.BlockSpec((1, GROUP_WIDTH), lambda b, s: (0, 0))],
        out_specs=pl.BlockSpec((None, ts, GROUP_WIDTH), lambda b, s: (b, s, 0)),
        out_shape=jax.ShapeDtypeStruct((B, S, GROUP_WIDTH), BF16),
        scratch_shapes=[pltpu.VMEM((POOL_HALO + ts, GROUP_WIDTH), F32)],
        compiler_params=_params(("parallel", "arbitrary")), name="pool_mixer")(u3, w_pool, pool_scale, gn_g)


def _conv_kernel(a_ref, gate_ref, wdw_ref, bdw_ref, lng_ref, lnb_ref, wpw_ref, gn_ref, o_ref,
                 buf_ref, zc_ref, *, ts, rc):
    s = pl.program_id(1)

    @pl.when(s == 0)
    def _():
        buf_ref[0:CONV_HALO, :] = jnp.zeros((CONV_HALO, GROUP_WIDTH), F32)

    @pl.when(s > 0)
    def _():
        buf_ref[0:CONV_HALO, :] = buf_ref[ts:ts + CONV_HALO, :]

    a = a_ref[...].astype(F32)
    gate = gate_ref[...].astype(F32)
    buf_ref[CONV_HALO:CONV_HALO + ts, :] = a * jax.nn.sigmoid(gate)
    first = CONV_HALO - (CONV_WIDTH - 1)

    sub = 8
    n_win = rc + CONV_HALO

    def lane_block(c, _):
        lanes = pl.ds(pl.multiple_of(c * LANE, LANE), LANE)
        for r0 in range(0, ts, rc):
            acc = jnp.broadcast_to(bdw_ref[:, lanes], (rc, LANE))
            win = buf_ref[r0:r0 + n_win, lanes]
            for b in range(sub):
                taps = [j for j in range(CONV_WIDTH) if (first + j) % sub == b]
                if b == 0:
                    for j in taps:
                        acc = acc + buf_ref[r0 + first + j:r0 + first + j + rc, lanes] * wdw_ref[j:j + 1, lanes]
                else:
                    shifted = pltpu.roll(win, n_win - b, 0)
                    for j in taps:
                        a = (first + j - b)
                        acc = acc + shifted[a:a + rc, :] * wdw_ref[j:j + 1, lanes]
            zc_ref[r0:r0 + rc, lanes] = acc
        return 0

    lax.fori_loop(0, GROUP_WIDTH // LANE, lane_block, 0)
    z = zc_ref[...]
    mu = jnp.mean(z, axis=-1, keepdims=True)
    zc = z - mu
    zn = zc * lax.rsqrt(jnp.mean(zc * zc, axis=-1, keepdims=True) + EPS) * lng_ref[...] + lnb_ref[...]
    act = zn * jax.nn.sigmoid(zn)
    y = jnp.dot(act.astype(BF16), wpw_ref[...], preferred_element_type=F32)
    o_ref[...] = _rms(y, gn_ref[...]).astype(o_ref.dtype)


def _conv(u3, w_dw, b_dw, ln_g, ln_b, w_pw, gn_g, ts=256, rc=64):
    B, S, _ = u3.shape
    vec = pl.BlockSpec((1, GROUP_WIDTH), lambda b, s: (0, 0))
    return pl.pallas_call(
        functools.partial(_conv_kernel, ts=ts, rc=rc),
        grid=(B, S // ts),
        in_specs=[pl.BlockSpec((None, ts, GROUP_WIDTH), lambda b, s: (b, s, U_CONV_A // GROUP_WIDTH)),
                  pl.BlockSpec((None, ts, GROUP_WIDTH), lambda b, s: (b, s, U_CONV_G // GROUP_WIDTH)),
                  pl.BlockSpec((CONV_HALO, GROUP_WIDTH), lambda b, s: (0, 0)),
                  vec, vec, vec,
                  pl.BlockSpec((GROUP_WIDTH, GROUP_WIDTH), lambda b, s: (0, 0)),
                  vec],
        out_specs=pl.BlockSpec((None, ts, GROUP_WIDTH), lambda b, s: (b, s, 0)),
        out_shape=jax.ShapeDtypeStruct((B, S, GROUP_WIDTH), BF16),
        scratch_shapes=[pltpu.VMEM((CONV_HALO + ts, GROUP_WIDTH), F32), pltpu.VMEM((ts, GROUP_WIDTH), F32)],
        compiler_params=_params(("parallel", "arbitrary")), name="conv_mixer")(
            u3, u3, w_dw, b_dw, ln_g, ln_b, w_pw, gn_g)


def _swa_kernel(sink_ref, q_ref, kp_ref, kc_ref, vp_ref, vc_ref, gn_ref, o_ref):
    n = pl.program_id(1)
    W = SWA_WINDOW
    dh = SWA_HEAD_DIM
    R = SWA_Q_HEADS // SWA_KV_HEADS
    q = q_ref[...] * (dh ** -0.5)
    k2 = jnp.concatenate([kp_ref[...], kc_ref[...]], axis=0)
    v2 = jnp.concatenate([vp_ref[...], vc_ref[...]], axis=0)
    qi = lax.broadcasted_iota(jnp.int32, (W, 2 * W), 0)
    kj = lax.broadcasted_iota(jnp.int32, (W, 2 * W), 1)
    rel = qi + W - kj
    valid = (rel >= 0) & (rel < W) & (n * W + kj - W >= 0)
    outs = []
    for g in range(SWA_KV_HEADS):
        kg = k2[:, g * dh:(g + 1) * dh]
        vg = v2[:, g * dh:(g + 1) * dh]
        for r in range(R):
            h = g * R + r
            s = lax.dot_general(q[:, h * dh:(h + 1) * dh], kg, (((1,), (1,)), ((), ())),
                                preferred_element_type=F32)
            s = jnp.where(valid, s, NEG_INF)
            sink = sink_ref[h]
            m = jnp.maximum(jnp.max(s, axis=-1, keepdims=True), sink)
            e = jnp.exp(s - m)
            denom = jnp.sum(e, axis=-1, keepdims=True) + jnp.exp(sink - m)
            p = e / denom
            outs.append(jnp.dot(p.astype(BF16), vg, preferred_element_type=F32))
    y = jnp.concatenate(outs, axis=-1)
    o_ref[...] = _rms(y, gn_ref[...]).astype(o_ref.dtype)


def _swa(u3, sinks, gn_g):
    B, S, _ = u3.shape
    W = SWA_WINDOW
    kcol, vcol = U_SWA_K // LANE, U_SWA_V // LANE

    def prev(col):
        return pl.BlockSpec((None, W, LANE), lambda b, n, sk: (b, jnp.maximum(n - 1, 0), col))

    def cur(col):
        return pl.BlockSpec((None, W, LANE), lambda b, n, sk: (b, n, col))

    grid_spec = pltpu.PrefetchScalarGridSpec(
        num_scalar_prefetch=1, grid=(B, S // W),
        in_specs=[pl.BlockSpec((None, W, GROUP_WIDTH), lambda b, n, sk: (b, n, U_SWA_Q // GROUP_WIDTH)),
                  prev(kcol), cur(kcol), prev(vcol), cur(vcol),
                  pl.BlockSpec((1, GROUP_WIDTH), lambda b, n, sk: (0, 0))],
        out_specs=pl.BlockSpec((None, W, GROUP_WIDTH), lambda b, n, sk: (b, n, 0)))
    return pl.pallas_call(
        _swa_kernel, grid_spec=grid_spec,
        out_shape=jax.ShapeDtypeStruct((B, S, GROUP_WIDTH), BF16),
        compiler_params=_params(("parallel", "arbitrary")), name="swa_mixer")(
            sinks, u3, u3, u3, u3, u3, gn_g)


def _router_kernel(x_ref, g_ref, w_ref, o_ref, cnt_ref, carry_ref, *, tm):
    @pl.when(pl.program_id(0) == 0)
    def _():
        carry_ref[...] = jnp.zeros_like(carry_ref)

    h = _rms(x_ref[...], g_ref[...])
    logits = jnp.dot(h.astype(BF16), w_ref[...], preferred_element_type=F32)
    lane = lax.broadcasted_iota(jnp.int32, (tm, LANE), 1).astype(F32)
    logits = jnp.where(lane < N_EXPERTS, logits, -jnp.inf)
    m1 = jnp.max(logits, axis=-1, keepdims=True)
    i1 = jnp.min(jnp.where(logits == m1, lane, float(LANE)), axis=-1, keepdims=True)
    rest = jnp.where(lane == i1, -jnp.inf, logits)
    m2 = jnp.max(rest, axis=-1, keepdims=True)
    i2 = jnp.min(jnp.where(rest == m2, lane, float(LANE)), axis=-1, keepdims=True)
    e2 = jnp.exp(m2 - m1)
    w1 = 1.0 / (1.0 + e2)
    w2 = e2 / (1.0 + e2)
    oh1 = (lane == i1).astype(F32)
    oh2 = (lane == i2).astype(F32)
    cnt = oh1 + oh2
    row = lax.broadcasted_iota(jnp.int32, (tm, tm), 0)
    col = lax.broadcasted_iota(jnp.int32, (tm, tm), 1)
    before = (row > col).astype(BF16)
    pre = jnp.dot(before, cnt.astype(BF16), preferred_element_type=F32) + carry_ref[0:1, :]
    r1 = jnp.sum(oh1 * pre, axis=-1, keepdims=True)
    r2 = jnp.sum(oh2 * pre, axis=-1, keepdims=True)
    carry_ref[0:1, :] = carry_ref[0:1, :] + jnp.sum(cnt, axis=0, keepdims=True)
    cols = (i1, i2, r1, r2, w1, w2)
    out = jnp.zeros((tm, LANE), F32)
    for ci, val in enumerate(cols):
        out = jnp.where(lane == ci, val, out)
    o_ref[...] = out
    cnt_ref[...] = jnp.broadcast_to(carry_ref[0:1, :], cnt_ref.shape)


def _router(x, g, w_router_pad, tm=256):
    T, D = x.shape
    return pl.pallas_call(
        functools.partial(_router_kernel, tm=tm),
        grid=(T // tm,),
        in_specs=[pl.BlockSpec((tm, D), lambda i: (i, 0)),
                  pl.BlockSpec((1, D), lambda i: (0, 0)),
                  pl.BlockSpec((D, LANE), lambda i: (0, 0))],
        out_specs=[pl.BlockSpec((tm, LANE), lambda i: (i, 0)),
                   pl.BlockSpec((8, LANE), lambda i: (0, 0))],
        out_shape=[jax.ShapeDtypeStruct((T, LANE), F32), jax.ShapeDtypeStruct((8, LANE), F32)],
        scratch_shapes=[pltpu.VMEM((8, LANE), F32)],
        compiler_params=_params(("arbitrary",)), name="router")(x, g, w_router_pad)


def _row_copy(src_hbm, row, dst, slot, sem):
    return pltpu.make_async_copy(src_hbm.at[pl.ds(row, 1), :], dst.at[pl.ds(slot, 1), :], sem)


GATHER_UNROLL = 8
GATHER_SLOTS = 2


def _dispatch_kernel(tok_ref, ns_ref, x_hbm, g_ref, o_ref, buf_ref, sem, *, rows, per_tile, n_blocks):
    i = pl.program_id(0)

    def live(j):
        return j % per_tile < ns_ref[j // per_tile]

    def gather(j, slot, wait):
        def body(r, _):
            cp = _row_copy(x_hbm, tok_ref[j * rows + r], buf_ref.at[slot], r, sem.at[slot])
            cp.wait() if wait else cp.start()
            return 0

        lax.fori_loop(0, rows, body, 0, unroll=GATHER_UNROLL)

    @pl.when((i == 0) & live(0))
    def _():
        gather(0, 0, False)

    nxt = jnp.minimum(i + 1, n_blocks - 1)

    @pl.when((i + 1 < n_blocks) & live(nxt))
    def _():
        gather(nxt, (i + 1) % GATHER_SLOTS, False)

    @pl.when(jnp.logical_not(live(i)))
    def _():
        o_ref[...] = jnp.zeros(o_ref.shape, o_ref.dtype)

    @pl.when(live(i))
    def _():
        slot = i % GATHER_SLOTS
        gather(i, slot, True)
        o_ref[...] = _rms(buf_ref[slot], g_ref[...]).astype(o_ref.dtype)


def _dispatch(x, g, row_token, tile_nsub, n_rows):
    T, D = x.shape
    rows = MOE_SUB
    n_blocks = n_rows // rows
    grid_spec = pltpu.PrefetchScalarGridSpec(
        num_scalar_prefetch=2, grid=(n_blocks,),
        in_specs=[pl.BlockSpec(memory_space=pl.ANY),
                  pl.BlockSpec((1, D), lambda i, tok, ns: (0, 0))],
        out_specs=pl.BlockSpec((rows, D), lambda i, tok, ns: (i, 0)),
        scratch_shapes=[pltpu.VMEM((GATHER_SLOTS, rows, D), F32), pltpu.SemaphoreType.DMA((GATHER_SLOTS,))])
    return pl.pallas_call(
        functools.partial(_dispatch_kernel, rows=rows, per_tile=MOE_TILE // MOE_SUB, n_blocks=n_blocks),
        grid_spec=grid_spec, out_shape=jax.ShapeDtypeStruct((n_rows, D), BF16),
        compiler_params=_params(("arbitrary",)), name="moe_dispatch")(row_token, tile_nsub, x, g)


def _combine_kernel(p0_ref, p1_ref, x_ref, gate_ref, eo_hbm, g_ref, o_ref, a_ref, b_ref, sem, *, rows, n_blocks):
    i = pl.program_id(0)

    def gather(j, slot, wait):
        def body(r, _):
            for p_ref, dst in ((p0_ref, a_ref), (p1_ref, b_ref)):
                cp = _row_copy(eo_hbm, p_ref[j * rows + r], dst.at[slot], r, sem.at[slot])
                cp.wait() if wait else cp.start()
            return 0

        lax.fori_loop(0, rows, body, 0, unroll=GATHER_UNROLL // 2)

    @pl.when(i == 0)
    def _():
        gather(0, 0, False)

    @pl.when(i + 1 < n_blocks)
    def _():
        gather(i + 1, (i + 1) % GATHER_SLOTS, False)

    slot = i % GATHER_SLOTS
    gather(i, slot, True)
    gate = gate_ref[...]
    y = x_ref[...] + (gate[:, 0:1] * a_ref[slot] + gate[:, 1:2] * b_ref[slot])
    o_ref[...] = _rms(y, g_ref[...]).astype(o_ref.dtype)


def _combine(x, eo, pos0, pos1, gate, g, rows=128):
    T, D = x.shape
    n_blocks = T // rows
    grid_spec = pltpu.PrefetchScalarGridSpec(
        num_scalar_prefetch=2, grid=(n_blocks,),
        in_specs=[pl.BlockSpec((rows, D), lambda i, p0, p1: (i, 0)),
                  pl.BlockSpec((rows, 2), lambda i, p0, p1: (i, 0)),
                  pl.BlockSpec(memory_space=pl.ANY),
                  pl.BlockSpec((1, D), lambda i, p0, p1: (0, 0))],
        out_specs=pl.BlockSpec((rows, D), lambda i, p0, p1: (i, 0)),
        scratch_shapes=[pltpu.VMEM((GATHER_SLOTS, rows, D), F32), pltpu.VMEM((GATHER_SLOTS, rows, D), F32),
                        pltpu.SemaphoreType.DMA((GATHER_SLOTS,))])
    return pl.pallas_call(
        functools.partial(_combine_kernel, rows=rows, n_blocks=n_blocks),
        grid_spec=grid_spec, out_shape=jax.ShapeDtypeStruct((T, D), F32),
        compiler_params=_params(("arbitrary",)), name="moe_combine")(pos0, pos1, x, gate, eo, g)


W_IN_SEGMENTS = (
    (1600, U_POOL, 1024),
    (2624, U_CONV_A, 2048),
    (4672, U_SWA_Q, 1024),
    (0, U_CQ, 1536),
    (1536, U_KPE, 64),
    (5696, U_SWA_K, 256),
)


def _pack_kernel(w_ref, o_ref):
    o_ref[...] = jnp.zeros(o_ref.shape, o_ref.dtype)
    for src, dst, width in W_IN_SEGMENTS:
        o_ref[:, dst:dst + width] = w_ref[:, src:src + width].astype(o_ref.dtype)


def _pack_w_in(w_in, tr=256):
    L, D, W = w_in.shape
    return pl.pallas_call(
        _pack_kernel, grid=(L, D // tr),
        in_specs=[pl.BlockSpec((None, tr, W), lambda l, r: (l, r, 0))],
        out_specs=pl.BlockSpec((None, tr, U_WIDTH), lambda l, r: (l, r, 0)),
        out_shape=jax.ShapeDtypeStruct((L, D, U_WIDTH), BF16),
        compiler_params=_params(("parallel", "parallel")), name="pack_w_in")(w_in)


def _rope_tables(S):
    inv = 1.0 / (ROPE_THETA ** (jnp.arange(0, MLA_ROPE, 2, dtype=F32) / MLA_ROPE))
    ang = jnp.arange(S, dtype=F32)[:, None] * inv[None, :]
    cos, sin = jnp.cos(ang), jnp.sin(ang)
    z32 = jnp.zeros_like(cos)
    z64 = jnp.zeros((S, 64), F32)
    c = jnp.concatenate([cos, cos, z64], axis=-1)
    sa = jnp.concatenate([-sin, z32, z64], axis=-1)
    sb = jnp.concatenate([z32, sin, z64], axis=-1)
    return c, sa, sb


def _moe_plan(route, counts, T, n_tiles):
    E = N_EXPERTS
    expert = route[:, 0:2].astype(jnp.int32)
    rank = route[:, 2:4].astype(jnp.int32)
    gate = route[:, 4:6]
    counts = counts.astype(jnp.int32)
    tiles_per = (counts + MOE_TILE - 1) // MOE_TILE
    tile_end = jnp.cumsum(tiles_per)
    tile_start = tile_end - tiles_per
    used = tile_end[E - 1]
    pos = tile_start[expert] * MOE_TILE + rank
    n_rows = n_tiles * MOE_TILE
    flat = pos.reshape(-1)
    token = jnp.repeat(jnp.arange(T, dtype=jnp.int32), 2)
    row_token = jnp.zeros((n_rows,), jnp.int32).at[flat].set(token, unique_indices=True)
    t = jnp.arange(n_tiles, dtype=jnp.int32)
    te = jnp.minimum(jnp.sum((t[:, None] >= tile_end[None, :]).astype(jnp.int32), axis=1), E - 1)
    last = jnp.maximum(used - 1, 0)
    te = jnp.where(t < used, te, te[last])
    live_rows = jnp.clip(counts[te] - (t - tile_start[te]) * MOE_TILE, 0, MOE_TILE)
    nsub = jnp.where(t < used, (live_rows + MOE_SUB - 1) // MOE_SUB, 0).astype(jnp.int32)
    src = jnp.minimum(t, last)
    return pos[:, 0], pos[:, 1], gate, row_token, (te.astype(jnp.int32), nsub, src)


def kernel(x, attn_norm_g, w_in, mla_q_norm_g, mla_w_q_up, mla_kv_norm_g, mla_w_kv_up, pool_w, pool_scale,
           conv_w_dw, conv_b_dw, conv_ln_g, conv_ln_b, conv_w_pw, swa_sinks, group_out_g, w_out, ffn_norm_g,
           dense_w_gate, dense_w_up, dense_w_down, moe_w_router, moe_w_gate, moe_w_up, moe_w_down, final_norm_g):
    B, S, D = x.shape
    T = B * S
    L = w_in.shape[0]
    H = MLA_HEADS
    x = x.reshape(T, D)

    w_in_p = _pack_w_in(w_in)
    wq = jnp.pad(mla_w_q_up.reshape(L, MLA_Q_LORA, H, MLA_QK), ((0, 0), (0, 0), (0, 0), (0, MLA_HEAD_PAD - MLA_QK)))
    wq = wq.reshape(L, MLA_Q_LORA, H * MLA_HEAD_PAD).astype(BF16)
    wkv = mla_w_kv_up.astype(BF16)
    tabs = _rope_tables(S)
    pool_w_b = pool_w.astype(BF16)
    conv_w_pw_b = conv_w_pw.astype(BF16)
    conv_w_dw_p = jnp.pad(conv_w_dw, ((0, 0), (0, CONV_HALO - CONV_WIDTH), (0, 0)))
    gn = group_out_g.reshape(L, 4, 1, GROUP_WIDTH)
    row = lambda v: v.reshape(1, -1).astype(F32)

    assert L == 2, "layer 0 dense FFN, layer 1 (last) expert FFN"
    n_dense_tiles = T // MOE_TILE
    dense_tiles = (jnp.zeros((n_dense_tiles,), jnp.int32), jnp.ones((n_dense_tiles,), jnp.int32),
                   jnp.arange(n_dense_tiles, dtype=jnp.int32))

    delta = None
    out = None
    for l in range(L):
        if delta is None:
            h = _norm(x, attn_norm_g[l])
        else:
            x, h = _norm(x, attn_norm_g[l], delta=delta, write_sum=True)
        u = _mm([h], w_in_p, l, tm=1024, tn=1024, out_dtype=BF16)
        u3 = u.reshape(B, S, U_WIDTH)
        q, k, v = _mla_project(u, B, S, row(mla_q_norm_g[l]), wq[l], row(mla_kv_norm_g[l]), wkv[l], tabs)
        y_a = _mla_attention(q, k, v).reshape(T, GROUP_WIDTH)
        y_a = _norm(y_a, gn[l, 0], out_dtype=BF16, tm=1024)
        y_b = _pool(u3, pool_w_b[l], row(pool_scale[l]), gn[l, 1]).reshape(T, GROUP_WIDTH)
        y_c = _conv(u3, conv_w_dw_p[l], row(conv_b_dw[l]), row(conv_ln_g[l]), row(conv_ln_b[l]),
                    conv_w_pw_b[l], gn[l, 2]).reshape(T, GROUP_WIDTH)
        y_d = _swa(u3, swa_sinks[l].astype(F32), gn[l, 3]).reshape(T, GROUP_WIDTH)
        x = _mm([y_a, y_b, y_c, y_d], w_out, l, tm=1024, tn=512, out_dtype=F32, res=x)
        i = l // 2
        if l % 2 == 0:
            h = _norm(x, ffn_norm_g[l])
            act = _swiglu(h, dense_w_gate, dense_w_up, dense_tiles, tm=MOE_TILE, tn=256, sub=MOE_TILE)
            delta = _down_dense(act, dense_w_down, i, tm=2048, tn=1024, tk=1536, tk_rem=256, out_dtype=BF16)
        else:
            g = row(ffn_norm_g[l])
            w_r = jnp.pad(moe_w_router[i], ((0, 0), (0, LANE - N_EXPERTS))).astype(BF16)
            route, counts = _router(x, g, w_r)
            n_tiles = (2 * T) // MOE_TILE + N_EXPERTS
            pos0, pos1, gate, row_token, tiles = _moe_plan(route, counts[0, :N_EXPERTS], T, n_tiles)
            xs = _dispatch(x, g, row_token, tiles[1], n_tiles * MOE_TILE)
            act = _swiglu(xs, moe_w_gate[i], moe_w_up[i], tiles, tm=MOE_TILE, tn=256, sub=MOE_SUB)
            eo = _down_grouped(act, moe_w_down[i], tiles, tm=MOE_TILE, tn=256, sub=MOE_SUB, out_dtype=F32)
            out = _combine(x, eo, pos0, pos1, gate, row(final_norm_g))
    return out.reshape(B, S, D)
```

```python
import functools

import jax
import jax.numpy as jnp
from jax import lax
from jax.experimental import pallas as pl
from jax.experimental.pallas import tpu as pltpu

F32 = jnp.float32
BF16 = jnp.bfloat16
EPS = 1e-6
NEG_INF = -1e30

GROUP_WIDTH = 1024
MLA_NOPE = 128
MLA_ROPE = 64
MLA_V = 128
MLA_HEADS = 8
MLA_QK = MLA_NOPE + MLA_ROPE
MLA_Q_LORA = 1024
MLA_KV_LORA = 512
MLA_HEAD_PAD = 256
ROPE_THETA = 10000.0
POOL_WINDOWS = (2, 4, 8, 16)
POOL_GROUP = 256
POOL_HALO = 16
CONV_WIDTH = 31
CONV_HALO = 32
SWA_HEAD_DIM = 64
SWA_Q_HEADS = 16
SWA_KV_HEADS = 2
SWA_WINDOW = 128
N_EXPERTS = 8

U_POOL = 0
U_CONV_A = 1024
U_CONV_G = 2048
U_SWA_Q = 3072
U_CQ = 4096
U_CKV = 5120
U_KPE = 5632
U_SWA_K = 5760
U_SWA_V = 5888
U_WIDTH = 6144

LANE = 128
MOE_TILE = 1280
MOE_SUB = 256
DENSE_TM = 1024
VMEM_LIMIT = 56 * 1024 * 1024


def _params(sem, vmem=VMEM_LIMIT):
    return pltpu.CompilerParams(dimension_semantics=sem, vmem_limit_bytes=vmem)


def _rms(x, g):
    return x * lax.rsqrt(jnp.mean(x * x, axis=-1, keepdims=True) + EPS) * g


def _norm_kernel(*refs, has_delta, write_sum, out_dtype):
    it = iter(refs)
    x_ref = next(it)
    d_ref = next(it) if has_delta else None
    g_ref = next(it)
    s_ref = next(it) if write_sum else None
    o_ref = next(it)
    x = x_ref[...]
    if has_delta:
        x = x + d_ref[...].astype(F32)
    if write_sum:
        s_ref[...] = x
    o_ref[...] = _rms(x, g_ref[...]).astype(out_dtype)


def _norm(x, g, delta=None, write_sum=False, out_dtype=BF16, tm=256):
    T, D = x.shape
    row = pl.BlockSpec((tm, D), lambda i: (i, 0))
    in_specs = [row] + ([row] if delta is not None else []) + [pl.BlockSpec((1, D), lambda i: (0, 0))]
    args = [x] + ([delta] if delta is not None else []) + [g.reshape(1, D).astype(F32)]
    out_shape = [jax.ShapeDtypeStruct((T, D), out_dtype)]
    out_specs = [row]
    if write_sum:
        out_shape = [jax.ShapeDtypeStruct((T, D), F32)] + out_shape
        out_specs = [row] + out_specs
    res = pl.pallas_call(
        functools.partial(_norm_kernel, has_delta=delta is not None, write_sum=write_sum, out_dtype=out_dtype),
        grid=(T // tm,), in_specs=in_specs, out_specs=out_specs, out_shape=out_shape,
        compiler_params=_params(("parallel",)), name="norm")(*args)
    return res if write_sum else res[0]


def _mm_kernel(*refs, nx, has_res):
    x_refs = refs[:nx]
    w_ref = refs[nx]
    res_ref = refs[nx + 1] if has_res else None
    o_ref = refs[-1]
    acc = None
    off = 0
    for xr in x_refs:
        kx = xr.shape[1]
        p = jnp.dot(xr[...], w_ref[off:off + kx, :].astype(BF16), preferred_element_type=F32)
        acc = p if acc is None else acc + p
        off += kx
    if has_res:
        acc = acc + res_ref[...]
    o_ref[...] = acc.astype(o_ref.dtype)


def _mm(xs, w3, g, tm, tn, out_dtype, res=None):
    M = xs[0].shape[0]
    _, K, N = w3.shape
    assert sum(x.shape[1] for x in xs) == K
    in_specs = [pl.BlockSpec((tm, x.shape[1]), lambda m, n: (m, 0)) for x in xs]
    in_specs.append(pl.BlockSpec((None, K, tn), lambda m, n: (g, 0, n)))
    args = list(xs) + [w3]
    if res is not None:
        in_specs.append(pl.BlockSpec((tm, tn), lambda m, n: (m, n)))
        args.append(res)
    return pl.pallas_call(
        functools.partial(_mm_kernel, nx=len(xs), has_res=res is not None),
        grid=(M // tm, N // tn), in_specs=in_specs,
        out_specs=pl.BlockSpec((tm, tn), lambda m, n: (m, n)),
        out_shape=jax.ShapeDtypeStruct((M, N), out_dtype),
        compiler_params=_params(("parallel", "arbitrary")), name="mm")(*args)


def _mm_nt_kernel(x_ref, wt_ref, o_ref):
    o_ref[...] = lax.dot_general(x_ref[...], wt_ref[...], (((1,), (1,)), ((), ())),
                                 preferred_element_type=F32).astype(o_ref.dtype)


def _mm_nt(x, wt3, g, tm, tn, out_dtype):
    M, K = x.shape
    _, N, _ = wt3.shape
    return pl.pallas_call(
        _mm_nt_kernel, grid=(M // tm, N // tn),
        in_specs=[pl.BlockSpec((tm, K), lambda m, n: (m, 0)),
                  pl.BlockSpec((None, tn, K), lambda m, n: (g, n, 0))],
        out_specs=pl.BlockSpec((tm, tn), lambda m, n: (m, n)),
        out_shape=jax.ShapeDtypeStruct((M, N), out_dtype),
        compiler_params=_params(("parallel", "arbitrary")), name="mm_nt")(x, wt3)


def _swiglu_kernel(te_ref, ns_ref, src_ref, x_ref, wg_ref, wu_ref, o_ref, *, nsub, sub):
    ns = ns_ref[pl.program_id(0)]
    for v in range(nsub + 1):
        @pl.when(ns == v)
        def _(v=v):
            rows = v * sub
            if v > 0:
                x = x_ref[:rows, :]
                g = jnp.dot(x, wg_ref[...].astype(BF16), preferred_element_type=F32)
                u = jnp.dot(x, wu_ref[...].astype(BF16), preferred_element_type=F32)
                o_ref[:rows, :] = (g * jax.nn.sigmoid(g) * u).astype(o_ref.dtype)
            if v < nsub:
                o_ref[rows:, :] = jnp.zeros((nsub * sub - rows, o_ref.shape[1]), o_ref.dtype)


def _swiglu(x, wg, wu, tiles, tm, tn, sub):
    M, K = x.shape
    _, _, N = wg.shape
    n_n = pl.cdiv(N, tn)
    n_m = M // tm

    def x_map(m, n, te, ns, src):
        return (src[m], 0)

    def w_map(m, n, te, ns, src):
        return (te[m], 0, jnp.where(ns[m] > 0, n, n_n - 1))

    grid_spec = pltpu.PrefetchScalarGridSpec(
        num_scalar_prefetch=3, grid=(n_m, n_n),
        in_specs=[pl.BlockSpec((tm, K), x_map),
                  pl.BlockSpec((None, K, tn), w_map),
                  pl.BlockSpec((None, K, tn), w_map)],
        out_specs=pl.BlockSpec((tm, tn), lambda m, n, te, ns, src: (m, n)))
    return pl.pallas_call(
        functools.partial(_swiglu_kernel, nsub=tm // sub, sub=sub),
        grid_spec=grid_spec, out_shape=jax.ShapeDtypeStruct((M, N), BF16),
        compiler_params=_params(("parallel", "arbitrary")), name="swiglu_up")(*tiles, x, wg, wu)


def _down_dense_kernel(xm_ref, xr_ref, wm_ref, wr_ref, o_ref, acc_ref, *, n_main):
    k = pl.program_id(2)

    @pl.when(k == 0)
    def _():
        acc_ref[...] = jnp.dot(xm_ref[...], wm_ref[...].astype(BF16), preferred_element_type=F32)

    @pl.when((k > 0) & (k < n_main))
    def _():
        acc_ref[...] += jnp.dot(xm_ref[...], wm_ref[...].astype(BF16), preferred_element_type=F32)

    @pl.when(k == n_main)
    def _():
        o_ref[...] = (acc_ref[...] + jnp.dot(xr_ref[...], wr_ref[...].astype(BF16),
                                             preferred_element_type=F32)).astype(o_ref.dtype)


def _down_dense(x, w3, g, tm, tn, tk, tk_rem, out_dtype):
    M, K = x.shape
    _, _, N = w3.shape
    n_main = (K - tk_rem) // tk
    assert n_main * tk + tk_rem == K and (K - tk_rem) % tk_rem == 0
    rem_idx = (K - tk_rem) // tk_rem

    def km(k):
        return jnp.minimum(k, n_main - 1)

    return pl.pallas_call(
        functools.partial(_down_dense_kernel, n_main=n_main),
        grid=(M // tm, N // tn, n_main + 1),
        in_specs=[pl.BlockSpec((tm, tk), lambda m, n, k: (m, km(k))),
                  pl.BlockSpec((tm, tk_rem), lambda m, n, k: (m, rem_idx)),
                  pl.BlockSpec((None, tk, tn), lambda m, n, k: (g, km(k), n)),
                  pl.BlockSpec((None, tk_rem, tn), lambda m, n, k: (g, rem_idx, n))],
        out_specs=pl.BlockSpec((tm, tn), lambda m, n, k: (m, n)),
        out_shape=jax.ShapeDtypeStruct((M, N), out_dtype),
        scratch_shapes=[pltpu.VMEM((tm, tn), F32)],
        compiler_params=_params(("parallel", "arbitrary", "arbitrary")), name="down_dense")(x, x, w3, w3)


def _down_grouped_kernel(te_ref, ns_ref, src_ref, x_ref, w_ref, o_ref, *, nsub, sub):
    ns = ns_ref[pl.program_id(0)]
    for v in range(nsub + 1):
        @pl.when(ns == v)
        def _(v=v):
            rows = v * sub
            if v > 0:
                o_ref[:rows, :] = jnp.dot(x_ref[:rows, :], w_ref[...].astype(BF16),
                                          preferred_element_type=F32).astype(o_ref.dtype)
            if v < nsub:
                o_ref[rows:, :] = jnp.zeros((nsub * sub - rows, o_ref.shape[1]), o_ref.dtype)


def _down_grouped(x, w3, tiles, tm, tn, sub, out_dtype):
    M, K = x.shape
    _, _, N = w3.shape
    n_n = N // tn

    def w_map(m, n, te, ns, src):
        return (te[m], 0, jnp.where(ns[m] > 0, n, n_n - 1))

    grid_spec = pltpu.PrefetchScalarGridSpec(
        num_scalar_prefetch=3, grid=(M // tm, n_n),
        in_specs=[pl.BlockSpec((tm, K), lambda m, n, te, ns, src: (src[m], 0)),
                  pl.BlockSpec((None, K, tn), w_map)],
        out_specs=pl.BlockSpec((tm, tn), lambda m, n, te, ns, src: (m, n)))
    return pl.pallas_call(
        functools.partial(_down_grouped_kernel, nsub=tm // sub, sub=sub),
        grid_spec=grid_spec, out_shape=jax.ShapeDtypeStruct((M, N), out_dtype),
        compiler_params=_params(("parallel", "arbitrary")), name="down_grouped")(*tiles, x, w3)


def _rope(x, c, sa, sb):
    return x * c + pltpu.roll(x, LANE - MLA_ROPE // 2, 1) * sa + pltpu.roll(x, MLA_ROPE // 2, 1) * sb


def _mla_project_kernel(cq_ref, ckv_ref, kpe_ref, qg_ref, kvg_ref, wq_ref, wkv_ref, c_ref, sa_ref, sb_ref,
                        q_ref, k_ref, v_ref, *, scale):
    c, sa, sb = c_ref[...], sa_ref[...], sb_ref[...]
    xq = _rms(cq_ref[...].astype(F32), qg_ref[...]).astype(BF16)
    rq = jnp.dot(xq, wq_ref[...], preferred_element_type=F32)
    xkv = _rms(ckv_ref[...].astype(F32), kvg_ref[...]).astype(BF16)
    rkv = jnp.dot(xkv, wkv_ref[...], preferred_element_type=F32)
    kpe = _rope(kpe_ref[...].astype(F32), c, sa, sb).astype(k_ref.dtype)
    for h in range(MLA_HEADS):
        lo = h * MLA_HEAD_PAD
        mid = lo + MLA_NOPE
        hi = lo + MLA_HEAD_PAD
        q_ref[h, :, :MLA_NOPE] = (rq[:, lo:mid] * scale).astype(q_ref.dtype)
        q_ref[h, :, MLA_NOPE:] = (_rope(rq[:, mid:hi], c, sa, sb) * scale).astype(q_ref.dtype)
        k_ref[h, :, :MLA_NOPE] = rkv[:, lo:mid].astype(k_ref.dtype)
        k_ref[h, :, MLA_NOPE:] = kpe
        v_ref[h, :, :] = rkv[:, mid:hi].astype(v_ref.dtype)


def _mla_project(u, B, S, q_g, wq, kv_g, wkv, tabs, tm=512):
    T = B * S
    n_s = S // tm
    H = MLA_HEADS
    tab_spec = pl.BlockSpec((tm, LANE), lambda m: (m % n_s, 0))

    def const(shape):
        return pl.BlockSpec(shape, lambda m: (0,) * len(shape))

    def head_spec(width):
        return pl.BlockSpec((None, H, tm, width), lambda m: (m // n_s, 0, m % n_s, 0))

    return pl.pallas_call(
        functools.partial(_mla_project_kernel, scale=MLA_QK ** -0.5),
        grid=(T // tm,),
        in_specs=[pl.BlockSpec((tm, MLA_Q_LORA), lambda m: (m, U_CQ // MLA_Q_LORA)),
                  pl.BlockSpec((tm, MLA_KV_LORA), lambda m: (m, U_CKV // MLA_KV_LORA)),
                  pl.BlockSpec((tm, LANE), lambda m: (m, U_KPE // LANE)),
                  const((1, MLA_Q_LORA)), const((1, MLA_KV_LORA)),
                  const((MLA_Q_LORA, H * MLA_HEAD_PAD)), const((MLA_KV_LORA, H * MLA_HEAD_PAD)),
                  tab_spec, tab_spec, tab_spec],
        out_specs=[head_spec(MLA_HEAD_PAD), head_spec(MLA_HEAD_PAD), head_spec(MLA_V)],
        out_shape=[jax.ShapeDtypeStruct((B, H, S, MLA_HEAD_PAD), BF16),
                   jax.ShapeDtypeStruct((B, H, S, MLA_HEAD_PAD), BF16),
                   jax.ShapeDtypeStruct((B, H, S, MLA_V), BF16)],
        compiler_params=_params(("parallel",)), name="mla_project")(u, u, u, q_g, kv_g, wq, wkv, *tabs)


def _flash_kernel(q_ref, k_ref, v_ref, o_ref, *, tq, tk, hp):
    qi = pl.program_id(2)

    nk = tq // tk

    def step(j, carry, diag):
        start = pl.multiple_of(j * tk, tk)
        out = []
        for hh in range(hp):
            m, l, acc = carry[hh]
            s = lax.dot_general(q_ref[hh], k_ref[hh, pl.ds(start, tk), :], (((1,), (1,)), ((), ())),
                                preferred_element_type=F32)
            if diag is not None:
                row = lax.broadcasted_iota(jnp.int32, (tq, tk), 0)
                col = lax.broadcasted_iota(jnp.int32, (tq, tk), 1) + diag * tk
                s = jnp.where(row >= col, s, NEG_INF)
            m_new = jnp.maximum(m, jnp.max(s, axis=-1, keepdims=True))
            alpha = jnp.exp(m - m_new)
            p = jnp.exp(s - m_new)
            l = alpha * l + jnp.sum(p, axis=-1, keepdims=True)
            acc = alpha * acc + jnp.dot(p.astype(BF16), v_ref[hh, pl.ds(start, tk), :],
                                        preferred_element_type=F32)
            out.append((m_new, l, acc))
        return tuple(out)

    carry = tuple((jnp.full((tq, 1), NEG_INF, F32), jnp.zeros((tq, 1), F32), jnp.zeros((tq, MLA_V), F32))
                  for _ in range(hp))
    def full_tile(t, c):
        for d in range(nk):
            c = step(t * nk + d, c, None)
        return c

    carry = lax.fori_loop(0, qi, full_tile, carry)
    for d in range(nk):
        carry = step(qi * nk + d, carry, d)
    for hh in range(hp):
        _, l, acc = carry[hh]
        o_ref[:, hh * MLA_V:(hh + 1) * MLA_V] = (acc / l).astype(o_ref.dtype)


def _mla_attention(q, k, v, tq=1024, tk=512, hp=1):
    B, H, S, _ = q.shape
    return pl.pallas_call(
        functools.partial(_flash_kernel, tq=tq, tk=tk, hp=hp),
        grid=(B, H // hp, S // tq),
        in_specs=[pl.BlockSpec((None, hp, tq, MLA_HEAD_PAD), lambda b, h, i: (b, h, i, 0)),
                  pl.BlockSpec((None, hp, S, MLA_HEAD_PAD), lambda b, h, i: (b, h, 0, 0)),
                  pl.BlockSpec((None, hp, S, MLA_V), lambda b, h, i: (b, h, 0, 0))],
        out_specs=pl.BlockSpec((None, tq, hp * MLA_V), lambda b, h, i: (b, i, h)),
        out_shape=jax.ShapeDtypeStruct((B, S, H * MLA_V), BF16),
        compiler_params=_params(("parallel", "parallel", "arbitrary")), name="mla_attention")(q, k, v)


def _pool_kernel(u_ref, w_ref, sc_ref, gn_ref, o_ref, buf_ref, *, ts):
    s = pl.program_id(1)

    @pl.when(s == 0)
    def _():
        buf_ref[0:POOL_HALO, :] = jnp.zeros((POOL_HALO, GROUP_WIDTH), F32)

    @pl.when(s > 0)
    def _():
        buf_ref[0:POOL_HALO, :] = buf_ref[ts:ts + POOL_HALO, :]

    buf_ref[POOL_HALO:POOL_HALO + ts, :] = u_ref[...].astype(F32)
    pos = s * ts + lax.broadcasted_iota(jnp.int32, (ts, 1), 0)
    ys = []
    ss = jnp.zeros((ts, 1), F32)
    for gi, w in enumerate(POOL_WINDOWS):
        lanes = slice(gi * POOL_GROUP, (gi + 1) * POOL_GROUP)
        cur = buf_ref[POOL_HALO:POOL_HALO + ts, lanes]
        win = cur
        for back in range(1, w):
            win = win + buf_ref[POOL_HALO - back:POOL_HALO - back + ts, lanes]
        count = jnp.minimum(pos + 1, w).astype(F32)
        d = win / count - cur
        y = jnp.dot(d.astype(BF16), w_ref[gi], preferred_element_type=F32) * sc_ref[:, lanes]
        ss = ss + jnp.sum(y * y, axis=-1, keepdims=True)
        ys.append(y)
    r = lax.rsqrt(ss / GROUP_WIDTH + EPS)
    for gi, y in enumerate(ys):
        lanes = slice(gi * POOL_GROUP, (gi + 1) * POOL_GROUP)
        o_ref[:, lanes] = (y * r * gn_ref[:, lanes]).astype(o_ref.dtype)


def _pool(u3, w_pool, pool_scale, gn_g, ts=512):
    B, S, _ = u3.shape
    return pl.pallas_call(
        functools.partial(_pool_kernel, ts=ts),
        grid=(B, S // ts),
        in_specs=[pl.BlockSpec((None, ts, GROUP_WIDTH), lambda b, s: (b, s, U_POOL // GROUP_WIDTH)),
                  pl.BlockSpec((len(POOL_WINDOWS), POOL_GROUP, POOL_GROUP), lambda b, s: (0, 0, 0)),
                  pl.BlockSpec((1, GROUP_WIDTH), lambda b, s: (0, 0)),
                  pl.BlockSpec((1, GROUP_WIDTH), lambda b, s: (0, 0))],
        out_specs=pl.BlockSpec((None, ts, GROUP_WIDTH), lambda b, s: (b, s, 0)),
        out_shape=jax.ShapeDtypeStruct((B, S, GROUP_WIDTH), BF16),
        scratch_shapes=[pltpu.VMEM((POOL_HALO + ts, GROUP_WIDTH), F32)],
        compiler_params=_params(("parallel", "arbitrary")), name="pool_mixer")(u3, w_pool, pool_scale, gn_g)


def _conv_kernel(a_ref, gate_ref, wdw_ref, bdw_ref, lng_ref, lnb_ref, wpw_ref, gn_ref, o_ref,
                 buf_ref, zc_ref, *, ts, rc):
    s = pl.program_id(1)

    @pl.when(s == 0)
    def _():
        buf_ref[0:CONV_HALO, :] = jnp.zeros((CONV_HALO, GROUP_WIDTH), F32)

    @pl.when(s > 0)
    def _():
        buf_ref[0:CONV_HALO, :] = buf_ref[ts:ts + CONV_HALO, :]

    a = a_ref[...].astype(F32)
    gate = gate_ref[...].astype(F32)
    buf_ref[CONV_HALO:CONV_HALO + ts, :] = a * jax.nn.sigmoid(gate)
    first = CONV_HALO - (CONV_WIDTH - 1)

    sub = 8
    n_win = rc + CONV_HALO

    def lane_block(c, _):
        lanes = pl.ds(pl.multiple_of(c * LANE, LANE), LANE)
        for r0 in range(0, ts, rc):
            acc = jnp.broadcast_to(bdw_ref[:, lanes], (rc, LANE))
            win = buf_ref[r0:r0 + n_win, lanes]
            for b in range(sub):
                taps = [j for j in range(CONV_WIDTH) if (first + j) % sub == b]
                if b == 0:
                    for j in taps:
                        acc = acc + buf_ref[r0 + first + j:r0 + first + j + rc, lanes] * wdw_ref[j:j + 1, lanes]
                else:
                    shifted = pltpu.roll(win, n_win - b, 0)
                    for j in taps:
                        a = (first + j - b)
                        acc = acc + shifted[a:a + rc, :] * wdw_ref[j:j + 1, lanes]
            zc_ref[r0:r0 + rc, lanes] = acc
        return 0

    lax.fori_loop(0, GROUP_WIDTH // LANE, lane_block, 0)
    z = zc_ref[...]
    mu = jnp.mean(z, axis=-1, keepdims=True)
    zc = z - mu
    zn = zc * lax.rsqrt(jnp.mean(zc * zc, axis=-1, keepdims=True) + EPS) * lng_ref[...] + lnb_ref[...]
    act = zn * jax.nn.sigmoid(zn)
    y = jnp.dot(act.astype(BF16), wpw_ref[...], preferred_element_type=F32)
    o_ref[...] = _rms(y, gn_ref[...]).astype(o_ref.dtype)


def _conv(u3, w_dw, b_dw, ln_g, ln_b, w_pw, gn_g, ts=256, rc=64):
    B, S, _ = u3.shape
    vec = pl.BlockSpec((1, GROUP_WIDTH), lambda b, s: (0, 0))
    return pl.pallas_call(
        functools.partial(_conv_kernel, ts=ts, rc=rc),
        grid=(B, S // ts),
        in_specs=[pl.BlockSpec((None, ts, GROUP_WIDTH), lambda b, s: (b, s, U_CONV_A // GROUP_WIDTH)),
                  pl.BlockSpec((None, ts, GROUP_WIDTH), lambda b, s: (b, s, U_CONV_G // GROUP_WIDTH)),
                  pl.BlockSpec((CONV_HALO, GROUP_WIDTH), lambda b, s: (0, 0)),
                  vec, vec, vec,
                  pl.BlockSpec((GROUP_WIDTH, GROUP_WIDTH), lambda b, s: (0, 0)),
                  vec],
        out_specs=pl.BlockSpec((None, ts, GROUP_WIDTH), lambda b, s: (b, s, 0)),
        out_shape=jax.ShapeDtypeStruct((B, S, GROUP_WIDTH), BF16),
        scratch_shapes=[pltpu.VMEM((CONV_HALO + ts, GROUP_WIDTH), F32), pltpu.VMEM((ts, GROUP_WIDTH), F32)],
        compiler_params=_params(("parallel", "arbitrary")), name="conv_mixer")(
            u3, u3, w_dw, b_dw, ln_g, ln_b, w_pw, gn_g)


def _swa_kernel(sink_ref, q_ref, kp_ref, kc_ref, vp_ref, vc_ref, gn_ref, o_ref):
    n = pl.program_id(1)
    W = SWA_WINDOW
    dh = SWA_HEAD_DIM
    R = SWA_Q_HEADS // SWA_KV_HEADS
    q = q_ref[...] * (dh ** -0.5)
    k2 = jnp.concatenate([kp_ref[...], kc_ref[...]], axis=0)
    v2 = jnp.concatenate([vp_ref[...], vc_ref[...]], axis=0)
    qi = lax.broadcasted_iota(jnp.int32, (W, 2 * W), 0)
    kj = lax.broadcasted_iota(jnp.int32, (W, 2 * W), 1)
    rel = qi + W - kj
    valid = (rel >= 0) & (rel < W) & (n * W + kj - W >= 0)
    outs = []
    for g in range(SWA_KV_HEADS):
        kg = k2[:, g * dh:(g + 1) * dh]
        vg = v2[:, g * dh:(g + 1) * dh]
        for r in range(R):
            h = g * R + r
            s = lax.dot_general(q[:, h * dh:(h + 1) * dh], kg, (((1,), (1,)), ((), ())),
                                preferred_element_type=F32)
            s = jnp.where(valid, s, NEG_INF)
            sink = sink_ref[h]
            m = jnp.maximum(jnp.max(s, axis=-1, keepdims=True), sink)
            e = jnp.exp(s - m)
            denom = jnp.sum(e, axis=-1, keepdims=True) + jnp.exp(sink - m)
            p = e / denom
            outs.append(jnp.dot(p.astype(BF16), vg, preferred_element_type=F32))
    y = jnp.concatenate(outs, axis=-1)
    o_ref[...] = _rms(y, gn_ref[...]).astype(o_ref.dtype)


def _swa(u3, sinks, gn_g):
    B, S, _ = u3.shape
    W = SWA_WINDOW
    kcol, vcol = U_SWA_K // LANE, U_SWA_V // LANE

    def prev(col):
        return pl.BlockSpec((None, W, LANE), lambda b, n, sk: (b, jnp.maximum(n - 1, 0), col))

    def cur(col):
        return pl.BlockSpec((None, W, LANE), lambda b, n, sk: (b, n, col))

    grid_spec = pltpu.PrefetchScalarGridSpec(
        num_scalar_prefetch=1, grid=(B, S // W),
        in_specs=[pl.BlockSpec((None, W, GROUP_WIDTH), lambda b, n, sk: (b, n, U_SWA_Q // GROUP_WIDTH)),
                  prev(kcol), cur(kcol), prev(vcol), cur(vcol),
                  pl.BlockSpec((1, GROUP_WIDTH), lambda b, n, sk: (0, 0))],
        out_specs=pl.BlockSpec((None, W, GROUP_WIDTH), lambda b, n, sk: (b, n, 0)))
    return pl.pallas_call(
        _swa_kernel, grid_spec=grid_spec,
        out_shape=jax.ShapeDtypeStruct((B, S, GROUP_WIDTH), BF16),
        compiler_params=_params(("parallel", "arbitrary")), name="swa_mixer")(
            sinks, u3, u3, u3, u3, u3, gn_g)


def _router_kernel(x_ref, g_ref, w_ref, o_ref, cnt_ref, carry_ref, *, tm):
    @pl.when(pl.program_id(0) == 0)
    def _():
        carry_ref[...] = jnp.zeros_like(carry_ref)

    h = _rms(x_ref[...], g_ref[...])
    logits = jnp.dot(h.astype(BF16), w_ref[...], preferred_element_type=F32)
    lane = lax.broadcasted_iota(jnp.int32, (tm, LANE), 1).astype(F32)
    logits = jnp.where(lane < N_EXPERTS, logits, -jnp.inf)
    m1 = jnp.max(logits, axis=-1, keepdims=True)
    i1 = jnp.min(jnp.where(logits == m1, lane, float(LANE)), axis=-1, keepdims=True)
    rest = jnp.where(lane == i1, -jnp.inf, logits)
    m2 = jnp.max(rest, axis=-1, keepdims=True)
    i2 = jnp.min(jnp.where(rest == m2, lane, float(LANE)), axis=-1, keepdims=True)
    e2 = jnp.exp(m2 - m1)
    w1 = 1.0 / (1.0 + e2)
    w2 = e2 / (1.0 + e2)
    oh1 = (lane == i1).astype(F32)
    oh2 = (lane == i2).astype(F32)
    cnt = oh1 + oh2
    row = lax.broadcasted_iota(jnp.int32, (tm, tm), 0)
    col = lax.broadcasted_iota(jnp.int32, (tm, tm), 1)
    before = (row > col).astype(BF16)
    pre = jnp.dot(before, cnt.astype(BF16), preferred_element_type=F32) + carry_ref[0:1, :]
    r1 = jnp.sum(oh1 * pre, axis=-1, keepdims=True)
    r2 = jnp.sum(oh2 * pre, axis=-1, keepdims=True)
    carry_ref[0:1, :] = carry_ref[0:1, :] + jnp.sum(cnt, axis=0, keepdims=True)
    cols = (i1, i2, r1, r2, w1, w2)
    out = jnp.zeros((tm, LANE), F32)
    for ci, val in enumerate(cols):
        out = jnp.where(lane == ci, val, out)
    o_ref[...] = out
    cnt_ref[...] = jnp.broadcast_to(carry_ref[0:1, :], cnt_ref.shape)


def _router(x, g, w_router_pad, tm=256):
    T, D = x.shape
    return pl.pallas_call(
        functools.partial(_router_kernel, tm=tm),
        grid=(T // tm,),
        in_specs=[pl.BlockSpec((tm, D), lambda i: (i, 0)),
                  pl.BlockSpec((1, D), lambda i: (0, 0)),
                  pl.BlockSpec((D, LANE), lambda i: (0, 0))],
        out_specs=[pl.BlockSpec((tm, LANE), lambda i: (i, 0)),
                   pl.BlockSpec((8, LANE), lambda i: (0, 0))],
        out_shape=[jax.ShapeDtypeStruct((T, LANE), F32), jax.ShapeDtypeStruct((8, LANE), F32)],
        scratch_shapes=[pltpu.VMEM((8, LANE), F32)],
        compiler_params=_params(("arbitrary",)), name="router")(x, g, w_router_pad)


def _row_copy(src_hbm, row, dst, slot, sem):
    return pltpu.make_async_copy(src_hbm.at[pl.ds(row, 1), :], dst.at[pl.ds(slot, 1), :], sem)


GATHER_UNROLL = 8
GATHER_SLOTS = 2


def _dispatch_kernel(tok_ref, ns_ref, x_hbm, g_ref, o_ref, buf_ref, sem, *, rows, per_tile, n_blocks):
    i = pl.program_id(0)

    def live(j):
        return j % per_tile < ns_ref[j // per_tile]

    def gather(j, slot, wait):
        def body(r, _):
            cp = _row_copy(x_hbm, tok_ref[j * rows + r], buf_ref.at[slot], r, sem.at[slot])
            cp.wait() if wait else cp.start()
            return 0

        lax.fori_loop(0, rows, body, 0, unroll=GATHER_UNROLL)

    @pl.when((i == 0) & live(0))
    def _():
        gather(0, 0, False)

    nxt = jnp.minimum(i + 1, n_blocks - 1)

    @pl.when((i + 1 < n_blocks) & live(nxt))
    def _():
        gather(nxt, (i + 1) % GATHER_SLOTS, False)

    @pl.when(jnp.logical_not(live(i)))
    def _():
        o_ref[...] = jnp.zeros(o_ref.shape, o_ref.dtype)

    @pl.when(live(i))
    def _():
        slot = i % GATHER_SLOTS
        gather(i, slot, True)
        o_ref[...] = _rms(buf_ref[slot], g_ref[...]).astype(o_ref.dtype)


def _dispatch(x, g, row_token, tile_nsub, n_rows):
    T, D = x.shape
    rows = MOE_SUB
    n_blocks = n_rows // rows
    grid_spec = pltpu.PrefetchScalarGridSpec(
        num_scalar_prefetch=2, grid=(n_blocks,),
        in_specs=[pl.BlockSpec(memory_space=pl.ANY),
                  pl.BlockSpec((1, D), lambda i, tok, ns: (0, 0))],
        out_specs=pl.BlockSpec((rows, D), lambda i, tok, ns: (i, 0)),
        scratch_shapes=[pltpu.VMEM((GATHER_SLOTS, rows, D), F32), pltpu.SemaphoreType.DMA((GATHER_SLOTS,))])
    return pl.pallas_call(
        functools.partial(_dispatch_kernel, rows=rows, per_tile=MOE_TILE // MOE_SUB, n_blocks=n_blocks),
        grid_spec=grid_spec, out_shape=jax.ShapeDtypeStruct((n_rows, D), BF16),
        compiler_params=_params(("arbitrary",)), name="moe_dispatch")(row_token, tile_nsub, x, g)


def _combine_kernel(p0_ref, p1_ref, x_ref, gate_ref, eo_hbm, g_ref, o_ref, a_ref, b_ref, sem, *, rows, n_blocks):
    i = pl.program_id(0)

    def gather(j, slot, wait):
        def body(r, _):
            for p_ref, dst in ((p0_ref, a_ref), (p1_ref, b_ref)):
                cp = _row_copy(eo_hbm, p_ref[j * rows + r], dst.at[slot], r, sem.at[slot])
                cp.wait() if wait else cp.start()
            return 0

        lax.fori_loop(0, rows, body, 0, unroll=GATHER_UNROLL // 2)

    @pl.when(i == 0)
    def _():
        gather(0, 0, False)

    @pl.when(i + 1 < n_blocks)
    def _():
        gather(i + 1, (i + 1) % GATHER_SLOTS, False)

    slot = i % GATHER_SLOTS
    gather(i, slot, True)
    gate = gate_ref[...]
    y = x_ref[...] + (gate[:, 0:1] * a_ref[slot] + gate[:, 1:2] * b_ref[slot])
    o_ref[...] = _rms(y, g_ref[...]).astype(o_ref.dtype)


def _combine(x, eo, pos0, pos1, gate, g, rows=128):
    T, D = x.shape
    n_blocks = T // rows
    grid_spec = pltpu.PrefetchScalarGridSpec(
        num_scalar_prefetch=2, grid=(n_blocks,),
        in_specs=[pl.BlockSpec((rows, D), lambda i, p0, p1: (i, 0)),
                  pl.BlockSpec((rows, 2), lambda i, p0, p1: (i, 0)),
                  pl.BlockSpec(memory_space=pl.ANY),
                  pl.BlockSpec((1, D), lambda i, p0, p1: (0, 0))],
        out_specs=pl.BlockSpec((rows, D), lambda i, p0, p1: (i, 0)),
        scratch_shapes=[pltpu.VMEM((GATHER_SLOTS, rows, D), F32), pltpu.VMEM((GATHER_SLOTS, rows, D), F32),
                        pltpu.SemaphoreType.DMA((GATHER_SLOTS,))])
    return pl.pallas_call(
        functools.partial(_combine_kernel, rows=rows, n_blocks=n_blocks),
        grid_spec=grid_spec, out_shape=jax.ShapeDtypeStruct((T, D), F32),
        compiler_params=_params(("arbitrary",)), name="moe_combine")(pos0, pos1, x, gate, eo, g)


W_IN_SEGMENTS = (
    (1600, U_POOL, 1024),
    (2624, U_CONV_A, 2048),
    (4672, U_SWA_Q, 1024),
    (0, U_CQ, 1536),
    (1536, U_KPE, 64),
    (5696, U_SWA_K, 256),
)


def _pack_kernel(w_ref, o_ref):
    o_ref[...] = jnp.zeros(o_ref.shape, o_ref.dtype)
    for src, dst, width in W_IN_SEGMENTS:
        o_ref[dst:dst + width, :] = w_ref[src:src + width, :].astype(o_ref.dtype)


def _pack_w_in(w_in_t, tc=256):
    L, W, D = w_in_t.shape
    return pl.pallas_call(
        _pack_kernel, grid=(L, D // tc),
        in_specs=[pl.BlockSpec((None, W, tc), lambda l, c: (l, 0, c))],
        out_specs=pl.BlockSpec((None, U_WIDTH, tc), lambda l, c: (l, 0, c)),
        out_shape=jax.ShapeDtypeStruct((L, U_WIDTH, D), BF16),
        compiler_params=_params(("parallel", "parallel")), name="pack_w_in")(w_in_t)


def _rope_tables(S):
    inv = 1.0 / (ROPE_THETA ** (jnp.arange(0, MLA_ROPE, 2, dtype=F32) / MLA_ROPE))
    ang = jnp.arange(S, dtype=F32)[:, None] * inv[None, :]
    cos, sin = jnp.cos(ang), jnp.sin(ang)
    z32 = jnp.zeros_like(cos)
    z64 = jnp.zeros((S, 64), F32)
    c = jnp.concatenate([cos, cos, z64], axis=-1)
    sa = jnp.concatenate([-sin, z32, z64], axis=-1)
    sb = jnp.concatenate([z32, sin, z64], axis=-1)
    return c, sa, sb


def _moe_plan(route, counts, T, n_tiles):
    E = N_EXPERTS
    expert = route[:, 0:2].astype(jnp.int32)
    rank = route[:, 2:4].astype(jnp.int32)
    gate = route[:, 4:6]
    counts = counts.astype(jnp.int32)
    tiles_per = (counts + MOE_TILE - 1) // MOE_TILE
    tile_end = jnp.cumsum(tiles_per)
    tile_start = tile_end - tiles_per
    used = tile_end[E - 1]
    pos = tile_start[expert] * MOE_TILE + rank
    n_rows = n_tiles * MOE_TILE
    flat = pos.reshape(-1)
    token = jnp.repeat(jnp.arange(T, dtype=jnp.int32), 2)
    row_token = jnp.zeros((n_rows,), jnp.int32).at[flat].set(token, unique_indices=True)
    t = jnp.arange(n_tiles, dtype=jnp.int32)
    te = jnp.minimum(jnp.sum((t[:, None] >= tile_end[None, :]).astype(jnp.int32), axis=1), E - 1)
    last = jnp.maximum(used - 1, 0)
    te = jnp.where(t < used, te, te[last])
    live_rows = jnp.clip(counts[te] - (t - tile_start[te]) * MOE_TILE, 0, MOE_TILE)
    nsub = jnp.where(t < used, (live_rows + MOE_SUB - 1) // MOE_SUB, 0).astype(jnp.int32)
    src = jnp.minimum(t, last)
    return pos[:, 0], pos[:, 1], gate, row_token, (te.astype(jnp.int32), nsub, src)


def kernel(x, attn_norm_g, w_in, mla_q_norm_g, mla_w_q_up, mla_kv_norm_g, mla_w_kv_up, pool_w, pool_scale,
           conv_w_dw, conv_b_dw, conv_ln_g, conv_ln_b, conv_w_pw, swa_sinks, group_out_g, w_out, ffn_norm_g,
           dense_w_gate, dense_w_up, dense_w_down, moe_w_router, moe_w_gate, moe_w_up, moe_w_down, final_norm_g):
    B, S, D = x.shape
    T = B * S
    L = w_in.shape[0]
    H = MLA_HEADS
    x = x.reshape(T, D)

    w_in_p = _pack_w_in(jnp.swapaxes(w_in, 1, 2))
    wq = jnp.pad(mla_w_q_up.reshape(L, MLA_Q_LORA, H, MLA_QK), ((0, 0), (0, 0), (0, 0), (0, MLA_HEAD_PAD - MLA_QK)))
    wq = wq.reshape(L, MLA_Q_LORA, H * MLA_HEAD_PAD).astype(BF16)
    wkv = mla_w_kv_up.astype(BF16)
    tabs = _rope_tables(S)
    pool_w_b = pool_w.astype(BF16)
    conv_w_pw_b = conv_w_pw.astype(BF16)
    conv_w_dw_p = jnp.pad(conv_w_dw, ((0, 0), (0, CONV_HALO - CONV_WIDTH), (0, 0)))
    gn = group_out_g.reshape(L, 4, 1, GROUP_WIDTH)
    row = lambda v: v.reshape(1, -1).astype(F32)

    assert L == 2, "layer 0 dense FFN, layer 1 (last) expert FFN"
    n_dense_tiles = T // DENSE_TM
    dense_tiles = (jnp.zeros((n_dense_tiles,), jnp.int32), jnp.ones((n_dense_tiles,), jnp.int32),
                   jnp.arange(n_dense_tiles, dtype=jnp.int32))

    delta = None
    out = None
    for l in range(L):
        if delta is None:
            h = _norm(x, attn_norm_g[l])
        else:
            x, h = _norm(x, attn_norm_g[l], delta=delta, write_sum=True)
        u = _mm_nt(h, w_in_p, l, tm=1024, tn=1024, out_dtype=BF16)
        u3 = u.reshape(B, S, U_WIDTH)
        q, k, v = _mla_project(u, B, S, row(mla_q_norm_g[l]), wq[l], row(mla_kv_norm_g[l]), wkv[l], tabs)
        y_a = _mla_attention(q, k, v).reshape(T, GROUP_WIDTH)
        y_a = _norm(y_a, gn[l, 0], out_dtype=BF16, tm=1024)
        y_b = _pool(u3, pool_w_b[l], row(pool_scale[l]), gn[l, 1]).reshape(T, GROUP_WIDTH)
        y_c = _conv(u3, conv_w_dw_p[l], row(conv_b_dw[l]), row(conv_ln_g[l]), row(conv_ln_b[l]),
                    conv_w_pw_b[l], gn[l, 2]).reshape(T, GROUP_WIDTH)
        y_d = _swa(u3, swa_sinks[l].astype(F32), gn[l, 3]).reshape(T, GROUP_WIDTH)
        x = _mm([y_a, y_b, y_c, y_d], w_out, l, tm=1024, tn=512, out_dtype=F32, res=x)
        i = l // 2
        if l % 2 == 0:
            h = _norm(x, ffn_norm_g[l])
            act = _swiglu(h, dense_w_gate, dense_w_up, dense_tiles, tm=DENSE_TM, tn=256, sub=DENSE_TM)
            delta = _down_dense(act, dense_w_down, i, tm=2048, tn=1024, tk=1536, tk_rem=256, out_dtype=BF16)
        else:
            g = row(ffn_norm_g[l])
            w_r = jnp.pad(moe_w_router[i], ((0, 0), (0, LANE - N_EXPERTS))).astype(BF16)
            route, counts = _router(x, g, w_r)
            n_tiles = pl.cdiv(2 * T, MOE_TILE) + N_EXPERTS
            pos0, pos1, gate, row_token, tiles = _moe_plan(route, counts[0, :N_EXPERTS], T, n_tiles)
            xs = _dispatch(x, g, row_token, tiles[1], n_tiles * MOE_TILE)
            act = _swiglu(xs, moe_w_gate[i], moe_w_up[i], tiles, tm=MOE_TILE, tn=256, sub=MOE_SUB)
            eo = _down_grouped(act, moe_w_down[i], tiles, tm=MOE_TILE, tn=256, sub=MOE_SUB, out_dtype=F32)
            out = _combine(x, eo, pos0, pos1, gate, row(final_norm_g))
    return out.reshape(B, S, D)
```

```python
import functools

import jax
import jax.numpy as jnp
from jax import lax
from jax.experimental import pallas as pl
from jax.experimental.pallas import tpu as pltpu

F32 = jnp.float32
BF16 = jnp.bfloat16
EPS = 1e-6
NEG_INF = -1e30
LOG2_E = 1.4426950408889634

GROUP_WIDTH = 1024
MLA_NOPE = 128
MLA_ROPE = 64
MLA_V = 128
MLA_HEADS = 8
MLA_QK = MLA_NOPE + MLA_ROPE
MLA_Q_LORA = 1024
MLA_KV_LORA = 512
MLA_HEAD_PAD = 256
ROPE_THETA = 10000.0
POOL_WINDOWS = (2, 4, 8, 16)
POOL_GROUP = 256
POOL_HALO = 16
CONV_WIDTH = 31
CONV_HALO = 32
SWA_HEAD_DIM = 64
SWA_Q_HEADS = 16
SWA_KV_HEADS = 2
SWA_WINDOW = 128
N_EXPERTS = 8

U_POOL = 0
U_CONV_A = 1024
U_CONV_G = 2048
U_SWA_Q = 3072
U_CQ = 4096
U_CKV = 5120
U_KPE = 5632
U_SWA_K = 5760
U_SWA_V = 5888
U_WIDTH = 6144

LANE = 128
MOE_TILE = 1280
MOE_SUB = 128
DENSE_TM = 1024
VMEM_LIMIT = 56 * 1024 * 1024


def _params(sem, vmem=VMEM_LIMIT):
    return pltpu.CompilerParams(dimension_semantics=sem, vmem_limit_bytes=vmem)


def _rms(x, g):
    return x * lax.rsqrt(jnp.mean(x * x, axis=-1, keepdims=True) + EPS) * g


def _norm_kernel(*refs, has_delta, write_sum, out_dtype):
    it = iter(refs)
    x_ref = next(it)
    d_ref = next(it) if has_delta else None
    g_ref = next(it)
    s_ref = next(it) if write_sum else None
    o_ref = next(it)
    x = x_ref[...]
    if has_delta:
        x = x + d_ref[...].astype(F32)
    if write_sum:
        s_ref[...] = x
    o_ref[...] = _rms(x, g_ref[...]).astype(out_dtype)


def _norm(x, g, delta=None, write_sum=False, out_dtype=BF16, tm=256):
    T, D = x.shape
    row = pl.BlockSpec((tm, D), lambda i: (i, 0))
    in_specs = [row] + ([row] if delta is not None else []) + [pl.BlockSpec((1, D), lambda i: (0, 0))]
    args = [x] + ([delta] if delta is not None else []) + [g.reshape(1, D).astype(F32)]
    out_shape = [jax.ShapeDtypeStruct((T, D), out_dtype)]
    out_specs = [row]
    if write_sum:
        out_shape = [jax.ShapeDtypeStruct((T, D), F32)] + out_shape
        out_specs = [row] + out_specs
    res = pl.pallas_call(
        functools.partial(_norm_kernel, has_delta=delta is not None, write_sum=write_sum, out_dtype=out_dtype),
        grid=(T // tm,), in_specs=in_specs, out_specs=out_specs, out_shape=out_shape,
        compiler_params=_params(("parallel",)), name="norm")(*args)
    return res if write_sum else res[0]


def _mm_kernel(*refs, nx, has_res):
    x_refs = refs[:nx]
    w_ref = refs[nx]
    res_ref = refs[nx + 1] if has_res else None
    o_ref = refs[-1]
    acc = None
    off = 0
    for xr in x_refs:
        kx = xr.shape[1]
        p = jnp.dot(xr[...], w_ref[off:off + kx, :].astype(BF16), preferred_element_type=F32)
        acc = p if acc is None else acc + p
        off += kx
    if has_res:
        acc = acc + res_ref[...]
    o_ref[...] = acc.astype(o_ref.dtype)


def _mm(xs, w3, g, tm, tn, out_dtype, res=None):
    M = xs[0].shape[0]
    _, K, N = w3.shape
    assert sum(x.shape[1] for x in xs) == K
    in_specs = [pl.BlockSpec((tm, x.shape[1]), lambda m, n: (m, 0)) for x in xs]
    in_specs.append(pl.BlockSpec((None, K, tn), lambda m, n: (g, 0, n)))
    args = list(xs) + [w3]
    if res is not None:
        in_specs.append(pl.BlockSpec((tm, tn), lambda m, n: (m, n)))
        args.append(res)
    return pl.pallas_call(
        functools.partial(_mm_kernel, nx=len(xs), has_res=res is not None),
        grid=(M // tm, N // tn), in_specs=in_specs,
        out_specs=pl.BlockSpec((tm, tn), lambda m, n: (m, n)),
        out_shape=jax.ShapeDtypeStruct((M, N), out_dtype),
        compiler_params=_params(("parallel", "arbitrary")), name="mm")(*args)


def _mm_nt_kernel(x_ref, wt_ref, o_ref):
    o_ref[...] = lax.dot_general(x_ref[...], wt_ref[...], (((1,), (1,)), ((), ())),
                                 preferred_element_type=F32).astype(o_ref.dtype)


def _mm_nt(x, wt3, g, tm, tn, out_dtype):
    M, K = x.shape
    _, N, _ = wt3.shape
    return pl.pallas_call(
        _mm_nt_kernel, grid=(M // tm, N // tn),
        in_specs=[pl.BlockSpec((tm, K), lambda m, n: (m, 0)),
                  pl.BlockSpec((None, tn, K), lambda m, n: (g, n, 0))],
        out_specs=pl.BlockSpec((tm, tn), lambda m, n: (m, n)),
        out_shape=jax.ShapeDtypeStruct((M, N), out_dtype),
        compiler_params=_params(("parallel", "arbitrary")), name="mm_nt")(x, wt3)


def _swiglu_kernel(te_ref, ns_ref, src_ref, x_ref, wg_ref, wu_ref, o_ref, *, nsub, sub):
    ns = ns_ref[pl.program_id(0)]
    for v in range(nsub + 1):
        @pl.when(ns == v)
        def _(v=v):
            rows = v * sub
            if v > 0:
                x = x_ref[:rows, :]
                g = jnp.dot(x, wg_ref[...].astype(BF16), preferred_element_type=F32)
                u = jnp.dot(x, wu_ref[...].astype(BF16), preferred_element_type=F32)
                o_ref[:rows, :] = (g * jax.nn.sigmoid(g) * u).astype(o_ref.dtype)
            if v < nsub:
                o_ref[rows:, :] = jnp.zeros((nsub * sub - rows, o_ref.shape[1]), o_ref.dtype)


def _swiglu(x, wg, wu, tiles, tm, tn, sub):
    M, K = x.shape
    _, _, N = wg.shape
    n_n = pl.cdiv(N, tn)
    n_m = M // tm

    def x_map(m, n, te, ns, src):
        return (src[m], 0)

    def w_map(m, n, te, ns, src):
        return (te[m], 0, jnp.where(ns[m] > 0, n, n_n - 1))

    grid_spec = pltpu.PrefetchScalarGridSpec(
        num_scalar_prefetch=3, grid=(n_m, n_n),
        in_specs=[pl.BlockSpec((tm, K), x_map),
                  pl.BlockSpec((None, K, tn), w_map),
                  pl.BlockSpec((None, K, tn), w_map)],
        out_specs=pl.BlockSpec((tm, tn), lambda m, n, te, ns, src: (m, n)))
    return pl.pallas_call(
        functools.partial(_swiglu_kernel, nsub=tm // sub, sub=sub),
        grid_spec=grid_spec, out_shape=jax.ShapeDtypeStruct((M, N), BF16),
        compiler_params=_params(("parallel", "arbitrary")), name="swiglu_up")(*tiles, x, wg, wu)


def _down_dense_kernel(xm_ref, xr_ref, wm_ref, wr_ref, o_ref, acc_ref, *, n_main):
    k = pl.program_id(2)

    @pl.when(k == 0)
    def _():
        acc_ref[...] = jnp.dot(xm_ref[...], wm_ref[...].astype(BF16), preferred_element_type=F32)

    @pl.when((k > 0) & (k < n_main))
    def _():
        acc_ref[...] += jnp.dot(xm_ref[...], wm_ref[...].astype(BF16), preferred_element_type=F32)

    @pl.when(k == n_main)
    def _():
        o_ref[...] = (acc_ref[...] + jnp.dot(xr_ref[...], wr_ref[...].astype(BF16),
                                             preferred_element_type=F32)).astype(o_ref.dtype)


def _down_dense(x, w3, g, tm, tn, tk, tk_rem, out_dtype):
    M, K = x.shape
    _, _, N = w3.shape
    n_main = (K - tk_rem) // tk
    assert n_main * tk + tk_rem == K and (K - tk_rem) % tk_rem == 0
    rem_idx = (K - tk_rem) // tk_rem

    def km(k):
        return jnp.minimum(k, n_main - 1)

    return pl.pallas_call(
        functools.partial(_down_dense_kernel, n_main=n_main),
        grid=(M // tm, N // tn, n_main + 1),
        in_specs=[pl.BlockSpec((tm, tk), lambda m, n, k: (m, km(k))),
                  pl.BlockSpec((tm, tk_rem), lambda m, n, k: (m, rem_idx)),
                  pl.BlockSpec((None, tk, tn), lambda m, n, k: (g, km(k), n)),
                  pl.BlockSpec((None, tk_rem, tn), lambda m, n, k: (g, rem_idx, n))],
        out_specs=pl.BlockSpec((tm, tn), lambda m, n, k: (m, n)),
        out_shape=jax.ShapeDtypeStruct((M, N), out_dtype),
        scratch_shapes=[pltpu.VMEM((tm, tn), F32)],
        compiler_params=_params(("parallel", "arbitrary", "arbitrary")), name="down_dense")(x, x, w3, w3)


def _down_grouped_kernel(te_ref, ns_ref, src_ref, x_ref, w_ref, o_ref, *, nsub, sub):
    ns = ns_ref[pl.program_id(0)]
    for v in range(nsub + 1):
        @pl.when(ns == v)
        def _(v=v):
            rows = v * sub
            if v > 0:
                o_ref[:rows, :] = jnp.dot(x_ref[:rows, :], w_ref[...].astype(BF16),
                                          preferred_element_type=F32).astype(o_ref.dtype)
            if v < nsub:
                o_ref[rows:, :] = jnp.zeros((nsub * sub - rows, o_ref.shape[1]), o_ref.dtype)


def _down_grouped(x, w3, tiles, tm, tn, sub, out_dtype):
    M, K = x.shape
    _, _, N = w3.shape
    n_n = N // tn

    def w_map(m, n, te, ns, src):
        return (te[m], 0, jnp.where(ns[m] > 0, n, n_n - 1))

    grid_spec = pltpu.PrefetchScalarGridSpec(
        num_scalar_prefetch=3, grid=(M // tm, n_n),
        in_specs=[pl.BlockSpec((tm, K), lambda m, n, te, ns, src: (src[m], 0)),
                  pl.BlockSpec((None, K, tn), w_map)],
        out_specs=pl.BlockSpec((tm, tn), lambda m, n, te, ns, src: (m, n)))
    return pl.pallas_call(
        functools.partial(_down_grouped_kernel, nsub=tm // sub, sub=sub),
        grid_spec=grid_spec, out_shape=jax.ShapeDtypeStruct((M, N), out_dtype),
        compiler_params=_params(("parallel", "arbitrary")), name="down_grouped")(*tiles, x, w3)


def _rope(x, c, sa, sb):
    return x * c + pltpu.roll(x, LANE - MLA_ROPE // 2, 1) * sa + pltpu.roll(x, MLA_ROPE // 2, 1) * sb


def _mla_project_kernel(cq_ref, ckv_ref, kpe_ref, qg_ref, kvg_ref, wq_ref, wkv_ref, c_ref, sa_ref, sb_ref,
                        q_ref, k_ref, v_ref, *, scale):
    c, sa, sb = c_ref[...], sa_ref[...], sb_ref[...]
    xq = _rms(cq_ref[...].astype(F32), qg_ref[...]).astype(BF16)
    rq = jnp.dot(xq, wq_ref[...], preferred_element_type=F32)
    xkv = _rms(ckv_ref[...].astype(F32), kvg_ref[...]).astype(BF16)
    rkv = jnp.dot(xkv, wkv_ref[...], preferred_element_type=F32)
    kpe = _rope(kpe_ref[...].astype(F32), c, sa, sb).astype(k_ref.dtype)
    for h in range(MLA_HEADS):
        lo = h * MLA_HEAD_PAD
        mid = lo + MLA_NOPE
        hi = lo + MLA_HEAD_PAD
        q_ref[h, :, :MLA_NOPE] = (rq[:, lo:mid] * scale).astype(q_ref.dtype)
        q_ref[h, :, MLA_NOPE:] = (_rope(rq[:, mid:hi], c, sa, sb) * scale).astype(q_ref.dtype)
        k_ref[h, :, :MLA_NOPE] = rkv[:, lo:mid].astype(k_ref.dtype)
        k_ref[h, :, MLA_NOPE:] = kpe
        v_ref[h, :, :] = rkv[:, mid:hi].astype(v_ref.dtype)


def _mla_project(u, B, S, q_g, wq, kv_g, wkv, tabs, tm=512):
    T = B * S
    n_s = S // tm
    H = MLA_HEADS
    tab_spec = pl.BlockSpec((tm, LANE), lambda m: (m % n_s, 0))

    def const(shape):
        return pl.BlockSpec(shape, lambda m: (0,) * len(shape))

    def head_spec(width):
        return pl.BlockSpec((None, H, tm, width), lambda m: (m // n_s, 0, m % n_s, 0))

    return pl.pallas_call(
        functools.partial(_mla_project_kernel, scale=MLA_QK ** -0.5 * LOG2_E),
        grid=(T // tm,),
        in_specs=[pl.BlockSpec((tm, MLA_Q_LORA), lambda m: (m, U_CQ // MLA_Q_LORA)),
                  pl.BlockSpec((tm, MLA_KV_LORA), lambda m: (m, U_CKV // MLA_KV_LORA)),
                  pl.BlockSpec((tm, LANE), lambda m: (m, U_KPE // LANE)),
                  const((1, MLA_Q_LORA)), const((1, MLA_KV_LORA)),
                  const((MLA_Q_LORA, H * MLA_HEAD_PAD)), const((MLA_KV_LORA, H * MLA_HEAD_PAD)),
                  tab_spec, tab_spec, tab_spec],
        out_specs=[head_spec(MLA_HEAD_PAD), head_spec(MLA_HEAD_PAD), head_spec(MLA_V)],
        out_shape=[jax.ShapeDtypeStruct((B, H, S, MLA_HEAD_PAD), BF16),
                   jax.ShapeDtypeStruct((B, H, S, MLA_HEAD_PAD), BF16),
                   jax.ShapeDtypeStruct((B, H, S, MLA_V), BF16)],
        compiler_params=_params(("parallel",)), name="mla_project")(u, u, u, q_g, kv_g, wq, wkv, *tabs)


def _flash_kernel(q_ref, k_ref, v_ref, o_ref, *, tq, tk, hp):
    qi = pl.program_id(2)

    nk = tq // tk

    def step(j, carry, diag):
        start = pl.multiple_of(j * tk, tk)
        out = []
        for hh in range(hp):
            m, l, acc = carry[hh]
            s = lax.dot_general(q_ref[hh], k_ref[hh, pl.ds(start, tk), :], (((1,), (1,)), ((), ())),
                                preferred_element_type=F32)
            if diag is not None:
                row = lax.broadcasted_iota(jnp.int32, (tq, tk), 0)
                col = lax.broadcasted_iota(jnp.int32, (tq, tk), 1) + diag * tk
                s = jnp.where(row >= col, s, NEG_INF)
            m_new = jnp.maximum(m, jnp.max(s, axis=-1, keepdims=True))
            alpha = jnp.exp2(m - m_new)
            p = jnp.exp2(s - m_new)
            l = alpha * l + jnp.sum(p, axis=-1, keepdims=True)
            acc = alpha * acc + jnp.dot(p.astype(BF16), v_ref[hh, pl.ds(start, tk), :],
                                        preferred_element_type=F32)
            out.append((m_new, l, acc))
        return tuple(out)

    carry = tuple((jnp.full((tq, 1), NEG_INF, F32), jnp.zeros((tq, 1), F32), jnp.zeros((tq, MLA_V), F32))
                  for _ in range(hp))
    def full_tile(t, c):
        for d in range(nk):
            c = step(t * nk + d, c, None)
        return c

    carry = lax.fori_loop(0, qi, full_tile, carry)
    for d in range(nk):
        carry = step(qi * nk + d, carry, d)
    for hh in range(hp):
        _, l, acc = carry[hh]
        o_ref[:, hh * MLA_V:(hh + 1) * MLA_V] = (acc / l).astype(o_ref.dtype)


def _mla_attention(q, k, v, tq=1024, tk=512, hp=1):
    B, H, S, _ = q.shape
    return pl.pallas_call(
        functools.partial(_flash_kernel, tq=tq, tk=tk, hp=hp),
        grid=(B, H // hp, S // tq),
        in_specs=[pl.BlockSpec((None, hp, tq, MLA_HEAD_PAD), lambda b, h, i: (b, h, i, 0)),
                  pl.BlockSpec((None, hp, S, MLA_HEAD_PAD), lambda b, h, i: (b, h, 0, 0)),
                  pl.BlockSpec((None, hp, S, MLA_V), lambda b, h, i: (b, h, 0, 0))],
        out_specs=pl.BlockSpec((None, tq, hp * MLA_V), lambda b, h, i: (b, i, h)),
        out_shape=jax.ShapeDtypeStruct((B, S, H * MLA_V), BF16),
        compiler_params=_params(("parallel", "parallel", "arbitrary")), name="mla_attention")(q, k, v)


def _pool_kernel(u_ref, w_ref, sc_ref, gn_ref, o_ref, buf_ref, *, ts):
    s = pl.program_id(1)

    @pl.when(s == 0)
    def _():
        buf_ref[0:POOL_HALO, :] = jnp.zeros((POOL_HALO, GROUP_WIDTH), F32)

    @pl.when(s > 0)
    def _():
        buf_ref[0:POOL_HALO, :] = buf_ref[ts:ts + POOL_HALO, :]

    buf_ref[POOL_HALO:POOL_HALO + ts, :] = u_ref[...].astype(F32)
    pos = s * ts + lax.broadcasted_iota(jnp.int32, (ts, 1), 0)
    ys = []
    ss = jnp.zeros((ts, 1), F32)
    for gi, w in enumerate(POOL_WINDOWS):
        lanes = slice(gi * POOL_GROUP, (gi + 1) * POOL_GROUP)
        cur = buf_ref[POOL_HALO:POOL_HALO + ts, lanes]
        win = cur
        for back in range(1, w):
            win = win + buf_ref[POOL_HALO - back:POOL_HALO - back + ts, lanes]
        count = jnp.minimum(pos + 1, w).astype(F32)
        d = win / count - cur
        y = jnp.dot(d.astype(BF16), w_ref[gi], preferred_element_type=F32) * sc_ref[:, lanes]
        ss = ss + jnp.sum(y * y, axis=-1, keepdims=True)
        ys.append(y)
    r = lax.rsqrt(ss / GROUP_WIDTH + EPS)
    for gi, y in enumerate(ys):
        lanes = slice(gi * POOL_GROUP, (gi + 1) * POOL_GROUP)
        o_ref[:, lanes] = (y * r * gn_ref[:, lanes]).astype(o_ref.dtype)


def _pool(u3, w_pool, pool_scale, gn_g, ts=512):
    B, S, _ = u3.shape
    return pl.pallas_call(
        functools.partial(_pool_kernel, ts=ts),
        grid=(B, S // ts),
        in_specs=[pl.BlockSpec((None, ts, GROUP_WIDTH), lambda b, s: (b, s, U_POOL // GROUP_WIDTH)),
                  pl.BlockSpec((len(POOL_WINDOWS), POOL_GROUP, POOL_GROUP), lambda b, s: (0, 0, 0)),
                  pl.BlockSpec((1, GROUP_WIDTH), lambda b, s: (0, 0)),
                  pl.BlockSpec((1, GROUP_WIDTH), lambda b, s: (0, 0))],
        out_specs=pl.BlockSpec((None, ts, GROUP_WIDTH), lambda b, s: (b, s, 0)),
        out_shape=jax.ShapeDtypeStruct((B, S, GROUP_WIDTH), BF16),
        scratch_shapes=[pltpu.VMEM((POOL_HALO + ts, GROUP_WIDTH), F32)],
        compiler_params=_params(("parallel", "arbitrary")), name="pool_mixer")(u3, w_pool, pool_scale, gn_g)


def _conv_kernel(a_ref, gate_ref, wdw_ref, bdw_ref, lng_ref, lnb_ref, wpw_ref, gn_ref, o_ref,
                 buf_ref, zc_ref, *, ts, rc):
    s = pl.program_id(1)

    @pl.when(s == 0)
    def _():
        buf_ref[0:CONV_HALO, :] = jnp.zeros((CONV_HALO, GROUP_WIDTH), F32)

    @pl.when(s > 0)
    def _():
        buf_ref[0:CONV_HALO, :] = buf_ref[ts:ts + CONV_HALO, :]

    a = a_ref[...].astype(F32)
    gate = gate_ref[...].astype(F32)
    buf_ref[CONV_HALO:CONV_HALO + ts, :] = a * jax.nn.sigmoid(gate)
    first = CONV_HALO - (CONV_WIDTH - 1)

    sub = 8
    n_win = rc + CONV_HALO

    def lane_block(c, _):
        lanes = pl.ds(pl.multiple_of(c * LANE, LANE), LANE)
        for r0 in range(0, ts, rc):
            acc = jnp.broadcast_to(bdw_ref[:, lanes], (rc, LANE))
            win = buf_ref[r0:r0 + n_win, lanes]
            for b in range(sub):
                taps = [j for j in range(CONV_WIDTH) if (first + j) % sub == b]
                if b == 0:
                    for j in taps:
                        acc = acc + buf_ref[r0 + first + j:r0 + first + j + rc, lanes] * wdw_ref[j:j + 1, lanes]
                else:
                    shifted = pltpu.roll(win, n_win - b, 0)
                    for j in taps:
                        a = (first + j - b)
                        acc = acc + shifted[a:a + rc, :] * wdw_ref[j:j + 1, lanes]
            zc_ref[r0:r0 + rc, lanes] = acc
        return 0

    lax.fori_loop(0, GROUP_WIDTH // LANE, lane_block, 0)
    z = zc_ref[...]
    mu = jnp.mean(z, axis=-1, keepdims=True)
    zc = z - mu
    zn = zc * lax.rsqrt(jnp.mean(zc * zc, axis=-1, keepdims=True) + EPS) * lng_ref[...] + lnb_ref[...]
    act = zn * jax.nn.sigmoid(zn)
    y = jnp.dot(act.astype(BF16), wpw_ref[...], preferred_element_type=F32)
    o_ref[...] = _rms(y, gn_ref[...]).astype(o_ref.dtype)


def _conv(u3, w_dw, b_dw, ln_g, ln_b, w_pw, gn_g, ts=256, rc=64):
    B, S, _ = u3.shape
    vec = pl.BlockSpec((1, GROUP_WIDTH), lambda b, s: (0, 0))
    return pl.pallas_call(
        functools.partial(_conv_kernel, ts=ts, rc=rc),
        grid=(B, S // ts),
        in_specs=[pl.BlockSpec((None, ts, GROUP_WIDTH), lambda b, s: (b, s, U_CONV_A // GROUP_WIDTH)),
                  pl.BlockSpec((None, ts, GROUP_WIDTH), lambda b, s: (b, s, U_CONV_G // GROUP_WIDTH)),
                  pl.BlockSpec((CONV_HALO, GROUP_WIDTH), lambda b, s: (0, 0)),
                  vec, vec, vec,
                  pl.BlockSpec((GROUP_WIDTH, GROUP_WIDTH), lambda b, s: (0, 0)),
                  vec],
        out_specs=pl.BlockSpec((None, ts, GROUP_WIDTH), lambda b, s: (b, s, 0)),
        out_shape=jax.ShapeDtypeStruct((B, S, GROUP_WIDTH), BF16),
        scratch_shapes=[pltpu.VMEM((CONV_HALO + ts, GROUP_WIDTH), F32), pltpu.VMEM((ts, GROUP_WIDTH), F32)],
        compiler_params=_params(("parallel", "arbitrary")), name="conv_mixer")(
            u3, u3, w_dw, b_dw, ln_g, ln_b, w_pw, gn_g)


def _swa_kernel(sink_ref, q_ref, kp_ref, kc_ref, vp_ref, vc_ref, gn_ref, o_ref):
    n = pl.program_id(1)
    W = SWA_WINDOW
    dh = SWA_HEAD_DIM
    R = SWA_Q_HEADS // SWA_KV_HEADS
    q = q_ref[...] * (dh ** -0.5)
    k2 = jnp.concatenate([kp_ref[...], kc_ref[...]], axis=0)
    v2 = jnp.concatenate([vp_ref[...], vc_ref[...]], axis=0)
    qi = lax.broadcasted_iota(jnp.int32, (W, 2 * W), 0)
    kj = lax.broadcasted_iota(jnp.int32, (W, 2 * W), 1)
    rel = qi + W - kj
    valid = (rel >= 0) & (rel < W) & (n * W + kj - W >= 0)
    outs = []
    for g in range(SWA_KV_HEADS):
        kg = k2[:, g * dh:(g + 1) * dh]
        vg = v2[:, g * dh:(g + 1) * dh]
        for r in range(R):
            h = g * R + r
            s = lax.dot_general(q[:, h * dh:(h + 1) * dh], kg, (((1,), (1,)), ((), ())),
                                preferred_element_type=F32)
            s = jnp.where(valid, s, NEG_INF)
            sink = sink_ref[h]
            m = jnp.maximum(jnp.max(s, axis=-1, keepdims=True), sink)
            e = jnp.exp(s - m)
            denom = jnp.sum(e, axis=-1, keepdims=True) + jnp.exp(sink - m)
            p = e / denom
            outs.append(jnp.dot(p.astype(BF16), vg, preferred_element_type=F32))
    y = jnp.concatenate(outs, axis=-1)
    o_ref[...] = _rms(y, gn_ref[...]).astype(o_ref.dtype)


def _swa(u3, sinks, gn_g):
    B, S, _ = u3.shape
    W = SWA_WINDOW
    kcol, vcol = U_SWA_K // LANE, U_SWA_V // LANE

    def prev(col):
        return pl.BlockSpec((None, W, LANE), lambda b, n, sk: (b, jnp.maximum(n - 1, 0), col))

    def cur(col):
        return pl.BlockSpec((None, W, LANE), lambda b, n, sk: (b, n, col))

    grid_spec = pltpu.PrefetchScalarGridSpec(
        num_scalar_prefetch=1, grid=(B, S // W),
        in_specs=[pl.BlockSpec((None, W, GROUP_WIDTH), lambda b, n, sk: (b, n, U_SWA_Q // GROUP_WIDTH)),
                  prev(kcol), cur(kcol), prev(vcol), cur(vcol),
                  pl.BlockSpec((1, GROUP_WIDTH), lambda b, n, sk: (0, 0))],
        out_specs=pl.BlockSpec((None, W, GROUP_WIDTH), lambda b, n, sk: (b, n, 0)))
    return pl.pallas_call(
        _swa_kernel, grid_spec=grid_spec,
        out_shape=jax.ShapeDtypeStruct((B, S, GROUP_WIDTH), BF16),
        compiler_params=_params(("parallel", "arbitrary")), name="swa_mixer")(
            sinks, u3, u3, u3, u3, u3, gn_g)


def _router_kernel(x_ref, g_ref, w_ref, o_ref, cnt_ref, carry_ref, *, tm):
    @pl.when(pl.program_id(0) == 0)
    def _():
        carry_ref[...] = jnp.zeros_like(carry_ref)

    h = _rms(x_ref[...], g_ref[...])
    logits = jnp.dot(h.astype(BF16), w_ref[...], preferred_element_type=F32)
    lane = lax.broadcasted_iota(jnp.int32, (tm, LANE), 1).astype(F32)
    logits = jnp.where(lane < N_EXPERTS, logits, -jnp.inf)
    m1 = jnp.max(logits, axis=-1, keepdims=True)
    i1 = jnp.min(jnp.where(logits == m1, lane, float(LANE)), axis=-1, keepdims=True)
    rest = jnp.where(lane == i1, -jnp.inf, logits)
    m2 = jnp.max(rest, axis=-1, keepdims=True)
    i2 = jnp.min(jnp.where(rest == m2, lane, float(LANE)), axis=-1, keepdims=True)
    e2 = jnp.exp(m2 - m1)
    w1 = 1.0 / (1.0 + e2)
    w2 = e2 / (1.0 + e2)
    oh1 = (lane == i1).astype(F32)
    oh2 = (lane == i2).astype(F32)
    cnt = oh1 + oh2
    row = lax.broadcasted_iota(jnp.int32, (tm, tm), 0)
    col = lax.broadcasted_iota(jnp.int32, (tm, tm), 1)
    before = (row > col).astype(BF16)
    pre = jnp.dot(before, cnt.astype(BF16), preferred_element_type=F32) + carry_ref[0:1, :]
    r1 = jnp.sum(oh1 * pre, axis=-1, keepdims=True)
    r2 = jnp.sum(oh2 * pre, axis=-1, keepdims=True)
    carry_ref[0:1, :] = carry_ref[0:1, :] + jnp.sum(cnt, axis=0, keepdims=True)
    cols = (i1, i2, r1, r2, w1, w2)
    out = jnp.zeros((tm, LANE), F32)
    for ci, val in enumerate(cols):
        out = jnp.where(lane == ci, val, out)
    o_ref[...] = out
    cnt_ref[...] = jnp.broadcast_to(carry_ref[0:1, :], cnt_ref.shape)


def _router(x, g, w_router_pad, tm=256):
    T, D = x.shape
    return pl.pallas_call(
        functools.partial(_router_kernel, tm=tm),
        grid=(T // tm,),
        in_specs=[pl.BlockSpec((tm, D), lambda i: (i, 0)),
                  pl.BlockSpec((1, D), lambda i: (0, 0)),
                  pl.BlockSpec((D, LANE), lambda i: (0, 0))],
        out_specs=[pl.BlockSpec((tm, LANE), lambda i: (i, 0)),
                   pl.BlockSpec((8, LANE), lambda i: (0, 0))],
        out_shape=[jax.ShapeDtypeStruct((T, LANE), F32), jax.ShapeDtypeStruct((8, LANE), F32)],
        scratch_shapes=[pltpu.VMEM((8, LANE), F32)],
        compiler_params=_params(("arbitrary",)), name="router")(x, g, w_router_pad)


def _row_copy(src_hbm, row, dst, slot, sem):
    return pltpu.make_async_copy(src_hbm.at[pl.ds(row, 1), :], dst.at[pl.ds(slot, 1), :], sem)


GATHER_UNROLL = 8
GATHER_SLOTS = 2
NORM_CHUNK = 16


def _dispatch_kernel(tok_ref, ns_ref, x_hbm, g_ref, o_ref, buf_ref, sem, *, rows, per_tile, n_blocks):
    i = pl.program_id(0)

    def live(j):
        return j % per_tile < ns_ref[j // per_tile]

    def gather(j, slot, wait):
        def body(r, _):
            cp = _row_copy(x_hbm, tok_ref[j * rows + r], buf_ref.at[slot], r, sem.at[slot])
            cp.wait() if wait else cp.start()
            return 0

        lax.fori_loop(0, rows, body, 0, unroll=GATHER_UNROLL)

    @pl.when((i == 0) & live(0))
    def _():
        gather(0, 0, False)

    nxt = jnp.minimum(i + 1, n_blocks - 1)

    @pl.when((i + 1 < n_blocks) & live(nxt))
    def _():
        gather(nxt, (i + 1) % GATHER_SLOTS, False)

    @pl.when(jnp.logical_not(live(i)))
    def _():
        o_ref[...] = jnp.zeros(o_ref.shape, o_ref.dtype)

    @pl.when(live(i))
    def _():
        slot = i % GATHER_SLOTS
        gather(i, slot, True)

        def norm_rows(c, _):
            r0 = pl.multiple_of(c * NORM_CHUNK, NORM_CHUNK)
            o_ref[pl.ds(r0, NORM_CHUNK), :] = _rms(buf_ref[slot, pl.ds(r0, NORM_CHUNK), :],
                                                   g_ref[...]).astype(o_ref.dtype)
            return 0

        lax.fori_loop(0, rows // NORM_CHUNK, norm_rows, 0, unroll=True)


def _dispatch(x, g, row_token, tile_nsub, n_rows):
    T, D = x.shape
    rows = MOE_SUB
    n_blocks = n_rows // rows
    grid_spec = pltpu.PrefetchScalarGridSpec(
        num_scalar_prefetch=2, grid=(n_blocks,),
        in_specs=[pl.BlockSpec(memory_space=pl.ANY),
                  pl.BlockSpec((1, D), lambda i, tok, ns: (0, 0))],
        out_specs=pl.BlockSpec((rows, D), lambda i, tok, ns: (i, 0)),
        scratch_shapes=[pltpu.VMEM((GATHER_SLOTS, rows, D), F32), pltpu.SemaphoreType.DMA((GATHER_SLOTS,))])
    return pl.pallas_call(
        functools.partial(_dispatch_kernel, rows=rows, per_tile=MOE_TILE // MOE_SUB, n_blocks=n_blocks),
        grid_spec=grid_spec, out_shape=jax.ShapeDtypeStruct((n_rows, D), BF16),
        compiler_params=_params(("arbitrary",)), name="moe_dispatch")(row_token, tile_nsub, x, g)


def _combine_kernel(p0_ref, p1_ref, x_ref, gate_ref, eo_hbm, g_ref, o_ref, a_ref, b_ref, sem, *, rows, n_blocks):
    i = pl.program_id(0)

    def gather(j, slot, wait):
        def body(r, _):
            for p_ref, dst in ((p0_ref, a_ref), (p1_ref, b_ref)):
                cp = _row_copy(eo_hbm, p_ref[j * rows + r], dst.at[slot], r, sem.at[slot])
                cp.wait() if wait else cp.start()
            return 0

        lax.fori_loop(0, rows, body, 0, unroll=GATHER_UNROLL // 2)

    @pl.when(i == 0)
    def _():
        gather(0, 0, False)

    @pl.when(i + 1 < n_blocks)
    def _():
        gather(i + 1, (i + 1) % GATHER_SLOTS, False)

    slot = i % GATHER_SLOTS
    gather(i, slot, True)

    def mix_rows(c, _):
        rs = pl.ds(pl.multiple_of(c * NORM_CHUNK, NORM_CHUNK), NORM_CHUNK)
        gate = gate_ref[rs, :]
        y = x_ref[rs, :] + (gate[:, 0:1] * a_ref[slot, rs, :] + gate[:, 1:2] * b_ref[slot, rs, :])
        o_ref[rs, :] = _rms(y, g_ref[...]).astype(o_ref.dtype)
        return 0

    lax.fori_loop(0, rows // NORM_CHUNK, mix_rows, 0, unroll=True)


def _combine(x, eo, pos0, pos1, gate, g, rows=128):
    T, D = x.shape
    n_blocks = T // rows
    grid_spec = pltpu.PrefetchScalarGridSpec(
        num_scalar_prefetch=2, grid=(n_blocks,),
        in_specs=[pl.BlockSpec((rows, D), lambda i, p0, p1: (i, 0)),
                  pl.BlockSpec((rows, 2), lambda i, p0, p1: (i, 0)),
                  pl.BlockSpec(memory_space=pl.ANY),
                  pl.BlockSpec((1, D), lambda i, p0, p1: (0, 0))],
        out_specs=pl.BlockSpec((rows, D), lambda i, p0, p1: (i, 0)),
        scratch_shapes=[pltpu.VMEM((GATHER_SLOTS, rows, D), F32), pltpu.VMEM((GATHER_SLOTS, rows, D), F32),
                        pltpu.SemaphoreType.DMA((GATHER_SLOTS,))])
    return pl.pallas_call(
        functools.partial(_combine_kernel, rows=rows, n_blocks=n_blocks),
        grid_spec=grid_spec, out_shape=jax.ShapeDtypeStruct((T, D), F32),
        compiler_params=_params(("arbitrary",)), name="moe_combine")(pos0, pos1, x, gate, eo, g)


W_IN_SEGMENTS = (
    (1600, U_POOL, 1024),
    (2624, U_CONV_A, 2048),
    (4672, U_SWA_Q, 1024),
    (0, U_CQ, 1536),
    (1536, U_KPE, 64),
    (5696, U_SWA_K, 256),
)


def _pack_kernel(w_ref, o_ref):
    o_ref[...] = jnp.zeros(o_ref.shape, o_ref.dtype)
    for src, dst, width in W_IN_SEGMENTS:
        o_ref[dst:dst + width, :] = w_ref[src:src + width, :].astype(o_ref.dtype)


def _pack_w_in(w_in_t, tc=256):
    L, W, D = w_in_t.shape
    return pl.pallas_call(
        _pack_kernel, grid=(L, D // tc),
        in_specs=[pl.BlockSpec((None, W, tc), lambda l, c: (l, 0, c))],
        out_specs=pl.BlockSpec((None, U_WIDTH, tc), lambda l, c: (l, 0, c)),
        out_shape=jax.ShapeDtypeStruct((L, U_WIDTH, D), BF16),
        compiler_params=_params(("parallel", "parallel")), name="pack_w_in")(w_in_t)


def _rope_tables(S):
    inv = 1.0 / (ROPE_THETA ** (jnp.arange(0, MLA_ROPE, 2, dtype=F32) / MLA_ROPE))
    ang = jnp.arange(S, dtype=F32)[:, None] * inv[None, :]
    cos, sin = jnp.cos(ang), jnp.sin(ang)
    z32 = jnp.zeros_like(cos)
    z64 = jnp.zeros((S, 64), F32)
    c = jnp.concatenate([cos, cos, z64], axis=-1)
    sa = jnp.concatenate([-sin, z32, z64], axis=-1)
    sb = jnp.concatenate([z32, sin, z64], axis=-1)
    return c, sa, sb


def _moe_plan(route, counts, T, n_tiles):
    E = N_EXPERTS
    expert = route[:, 0:2].astype(jnp.int32)
    rank = route[:, 2:4].astype(jnp.int32)
    gate = route[:, 4:6]
    counts = counts.astype(jnp.int32)
    tiles_per = (counts + MOE_TILE - 1) // MOE_TILE
    tile_end = jnp.cumsum(tiles_per)
    tile_start = tile_end - tiles_per
    used = tile_end[E - 1]
    even = (counts + jnp.maximum(tiles_per, 1) - 1) // jnp.maximum(tiles_per, 1)
    per = (even + MOE_SUB - 1) // MOE_SUB * MOE_SUB
    per = jnp.maximum(per, MOE_SUB)
    per_tok = per[expert]
    pos = (tile_start[expert] + rank // per_tok) * MOE_TILE + rank % per_tok
    n_rows = n_tiles * MOE_TILE
    flat = pos.reshape(-1)
    token = jnp.repeat(jnp.arange(T, dtype=jnp.int32), 2)
    row_token = jnp.zeros((n_rows,), jnp.int32).at[flat].set(token, unique_indices=True)
    t = jnp.arange(n_tiles, dtype=jnp.int32)
    te = jnp.minimum(jnp.sum((t[:, None] >= tile_end[None, :]).astype(jnp.int32), axis=1), E - 1)
    last = jnp.maximum(used - 1, 0)
    te = jnp.where(t < used, te, te[last])
    live_rows = jnp.clip(counts[te] - (t - tile_start[te]) * per[te], 0, per[te])
    nsub = jnp.where(t < used, (live_rows + MOE_SUB - 1) // MOE_SUB, 0).astype(jnp.int32)
    src = jnp.minimum(t, last)
    return pos[:, 0], pos[:, 1], gate, row_token, (te.astype(jnp.int32), nsub, src)


def kernel(x, attn_norm_g, w_in, mla_q_norm_g, mla_w_q_up, mla_kv_norm_g, mla_w_kv_up, pool_w, pool_scale,
           conv_w_dw, conv_b_dw, conv_ln_g, conv_ln_b, conv_w_pw, swa_sinks, group_out_g, w_out, ffn_norm_g,
           dense_w_gate, dense_w_up, dense_w_down, moe_w_router, moe_w_gate, moe_w_up, moe_w_down, final_norm_g):
    B, S, D = x.shape
    T = B * S
    L = w_in.shape[0]
    H = MLA_HEADS
    x = x.reshape(T, D)

    w_in_p = _pack_w_in(jnp.swapaxes(w_in, 1, 2))
    wq = jnp.pad(mla_w_q_up.reshape(L, MLA_Q_LORA, H, MLA_QK), ((0, 0), (0, 0), (0, 0), (0, MLA_HEAD_PAD - MLA_QK)))
    wq = wq.reshape(L, MLA_Q_LORA, H * MLA_HEAD_PAD).astype(BF16)
    wkv = mla_w_kv_up.astype(BF16)
    tabs = _rope_tables(S)
    pool_w_b = pool_w.astype(BF16)
    conv_w_pw_b = conv_w_pw.astype(BF16)
    conv_w_dw_p = jnp.pad(conv_w_dw, ((0, 0), (0, CONV_HALO - CONV_WIDTH), (0, 0)))
    gn = group_out_g.reshape(L, 4, 1, GROUP_WIDTH)
    row = lambda v: v.reshape(1, -1).astype(F32)

    assert L == 2, "layer 0 dense FFN, layer 1 (last) expert FFN"
    n_dense_tiles = T // DENSE_TM
    dense_tiles = (jnp.zeros((n_dense_tiles,), jnp.int32), jnp.ones((n_dense_tiles,), jnp.int32),
                   jnp.arange(n_dense_tiles, dtype=jnp.int32))

    delta = None
    out = None
    for l in range(L):
        if delta is None:
            h = _norm(x, attn_norm_g[l])
        else:
            x, h = _norm(x, attn_norm_g[l], delta=delta, write_sum=True)
        u = _mm_nt(h, w_in_p, l, tm=1024, tn=1024, out_dtype=BF16)
        u3 = u.reshape(B, S, U_WIDTH)
        q, k, v = _mla_project(u, B, S, row(mla_q_norm_g[l]), wq[l], row(mla_kv_norm_g[l]), wkv[l], tabs)
        y_a = _mla_attention(q, k, v).reshape(T, GROUP_WIDTH)
        y_a = _norm(y_a, gn[l, 0], out_dtype=BF16, tm=1024)
        y_b = _pool(u3, pool_w_b[l], row(pool_scale[l]), gn[l, 1]).reshape(T, GROUP_WIDTH)
        y_c = _conv(u3, conv_w_dw_p[l], row(conv_b_dw[l]), row(conv_ln_g[l]), row(conv_ln_b[l]),
                    conv_w_pw_b[l], gn[l, 2]).reshape(T, GROUP_WIDTH)
        y_d = _swa(u3, swa_sinks[l].astype(F32), gn[l, 3]).reshape(T, GROUP_WIDTH)
        x = _mm([y_a, y_b, y_c, y_d], w_out, l, tm=1024, tn=512, out_dtype=F32, res=x)
        i = l // 2
        if l % 2 == 0:
            h = _norm(x, ffn_norm_g[l])
            act = _swiglu(h, dense_w_gate, dense_w_up, dense_tiles, tm=DENSE_TM, tn=256, sub=DENSE_TM)
            delta = _down_dense(act, dense_w_down, i, tm=2048, tn=1024, tk=1536, tk_rem=256, out_dtype=BF16)
        else:
            g = row(ffn_norm_g[l])
            w_r = jnp.pad(moe_w_router[i], ((0, 0), (0, LANE - N_EXPERTS))).astype(BF16)
            route, counts = _router(x, g, w_r)
            n_tiles = pl.cdiv(2 * T, MOE_TILE) + N_EXPERTS
            pos0, pos1, gate, row_token, tiles = _moe_plan(route, counts[0, :N_EXPERTS], T, n_tiles)
            xs = _dispatch(x, g, row_token, tiles[1], n_tiles * MOE_TILE)
            act = _swiglu(xs, moe_w_gate[i], moe_w_up[i], tiles, tm=MOE_TILE, tn=256, sub=MOE_SUB)
            eo = _down_grouped(act, moe_w_down[i], tiles, tm=MOE_TILE, tn=256, sub=MOE_SUB, out_dtype=F32)
            out = _combine(x, eo, pos0, pos1, gate, row(final_norm_g))
    return out.reshape(B, S, D)
```

```python
import functools

import jax
import jax.numpy as jnp
from jax import lax
from jax.experimental import pallas as pl
from jax.experimental.pallas import tpu as pltpu

F32 = jnp.float32
BF16 = jnp.bfloat16
EPS = 1e-6
NEG_INF = -1e30
LOG2_E = 1.4426950408889634

GROUP_WIDTH = 1024
MLA_NOPE = 128
MLA_ROPE = 64
MLA_V = 128
MLA_HEADS = 8
MLA_QK = MLA_NOPE + MLA_ROPE
MLA_Q_LORA = 1024
MLA_KV_LORA = 512
MLA_HEAD_PAD = 256
ROPE_THETA = 10000.0
POOL_WINDOWS = (2, 4, 8, 16)
POOL_GROUP = 256
POOL_HALO = 16
CONV_WIDTH = 31
CONV_HALO = 32
SWA_HEAD_DIM = 64
SWA_Q_HEADS = 16
SWA_KV_HEADS = 2
SWA_WINDOW = 128
N_EXPERTS = 8

U_POOL = 0
U_CONV_A = 1024
U_CONV_G = 2048
U_SWA_Q = 3072
U_CQ = 4096
U_CKV = 5120
U_KPE = 5632
U_SWA_K = 5760
U_SWA_V = 5888
U_WIDTH = 6144

LANE = 128
V7X_VMEM_BYTES = 64 * 1024 * 1024
VMEM_LIMIT = V7X_VMEM_BYTES * 7 // 8

MOE_TILE = 1280
MOE_SUB = 128
DENSE_TM = 1024
FFN_UP_TN = 256
MM_IN = dict(tm=1024, tn=1024)
MM_OUT = dict(tm=1024, tn=512)
DOWN_DENSE = dict(tm=2048, tn=1024, tk=1536, tk_rem=256)
DOWN_GROUPED_TN = 256


def _params(sem, vmem=VMEM_LIMIT):
    return pltpu.CompilerParams(dimension_semantics=sem, vmem_limit_bytes=vmem)


def _rms(x, g):
    return x * lax.rsqrt(jnp.mean(x * x, axis=-1, keepdims=True) + EPS) * g


def _norm_kernel(*refs, has_delta, write_sum, out_dtype):
    it = iter(refs)
    x_ref = next(it)
    d_ref = next(it) if has_delta else None
    g_ref = next(it)
    s_ref = next(it) if write_sum else None
    o_ref = next(it)
    x = x_ref[...]
    if has_delta:
        x = x + d_ref[...].astype(F32)
    if write_sum:
        s_ref[...] = x
    o_ref[...] = _rms(x, g_ref[...]).astype(out_dtype)


def _norm(x, g, delta=None, write_sum=False, out_dtype=BF16, tm=256):
    T, D = x.shape
    row = pl.BlockSpec((tm, D), lambda i: (i, 0))
    in_specs = [row] + ([row] if delta is not None else []) + [pl.BlockSpec((1, D), lambda i: (0, 0))]
    args = [x] + ([delta] if delta is not None else []) + [g.reshape(1, D).astype(F32)]
    out_shape = [jax.ShapeDtypeStruct((T, D), out_dtype)]
    out_specs = [row]
    if write_sum:
        out_shape = [jax.ShapeDtypeStruct((T, D), F32)] + out_shape
        out_specs = [row] + out_specs
    res = pl.pallas_call(
        functools.partial(_norm_kernel, has_delta=delta is not None, write_sum=write_sum, out_dtype=out_dtype),
        grid=(T // tm,), in_specs=in_specs, out_specs=out_specs, out_shape=out_shape,
        compiler_params=_params(("parallel",)), name="norm")(*args)
    return res if write_sum else res[0]


def _mm_kernel(*refs, nx, has_res):
    x_refs = refs[:nx]
    w_ref = refs[nx]
    res_ref = refs[nx + 1] if has_res else None
    o_ref = refs[-1]
    acc = None
    off = 0
    for xr in x_refs:
        kx = xr.shape[1]
        p = jnp.dot(xr[...], w_ref[off:off + kx, :].astype(BF16), preferred_element_type=F32)
        acc = p if acc is None else acc + p
        off += kx
    if has_res:
        acc = acc + res_ref[...]
    o_ref[...] = acc.astype(o_ref.dtype)


def _mm(xs, w3, g, tm, tn, out_dtype, res=None):
    M = xs[0].shape[0]
    _, K, N = w3.shape
    assert sum(x.shape[1] for x in xs) == K
    in_specs = [pl.BlockSpec((tm, x.shape[1]), lambda m, n: (m, 0)) for x in xs]
    in_specs.append(pl.BlockSpec((None, K, tn), lambda m, n: (g, 0, n)))
    args = list(xs) + [w3]
    if res is not None:
        in_specs.append(pl.BlockSpec((tm, tn), lambda m, n: (m, n)))
        args.append(res)
    return pl.pallas_call(
        functools.partial(_mm_kernel, nx=len(xs), has_res=res is not None),
        grid=(M // tm, N // tn), in_specs=in_specs,
        out_specs=pl.BlockSpec((tm, tn), lambda m, n: (m, n)),
        out_shape=jax.ShapeDtypeStruct((M, N), out_dtype),
        compiler_params=_params(("parallel", "arbitrary")), name="mm")(*args)


def _mm_nt_kernel(x_ref, wt_ref, o_ref):
    o_ref[...] = lax.dot_general(x_ref[...], wt_ref[...], (((1,), (1,)), ((), ())),
                                 preferred_element_type=F32).astype(o_ref.dtype)


def _mm_nt(x, wt3, g, tm, tn, out_dtype):
    M, K = x.shape
    _, N, _ = wt3.shape
    return pl.pallas_call(
        _mm_nt_kernel, grid=(M // tm, N // tn),
        in_specs=[pl.BlockSpec((tm, K), lambda m, n: (m, 0)),
                  pl.BlockSpec((None, tn, K), lambda m, n: (g, n, 0))],
        out_specs=pl.BlockSpec((tm, tn), lambda m, n: (m, n)),
        out_shape=jax.ShapeDtypeStruct((M, N), out_dtype),
        compiler_params=_params(("parallel", "arbitrary")), name="mm_nt")(x, wt3)


def _swiglu_kernel(te_ref, ns_ref, src_ref, x_ref, wg_ref, wu_ref, o_ref, *, nsub, sub):
    ns = ns_ref[pl.program_id(0)]
    for v in range(nsub + 1):
        @pl.when(ns == v)
        def _(v=v):
            rows = v * sub
            if v > 0:
                x = x_ref[:rows, :]
                g = jnp.dot(x, wg_ref[...].astype(BF16), preferred_element_type=F32)
                u = jnp.dot(x, wu_ref[...].astype(BF16), preferred_element_type=F32)
                o_ref[:rows, :] = (g * jax.nn.sigmoid(g) * u).astype(o_ref.dtype)
            if v < nsub:
                o_ref[rows:, :] = jnp.zeros((nsub * sub - rows, o_ref.shape[1]), o_ref.dtype)


def _swiglu(x, wg, wu, tiles, tm, tn, sub):
    M, K = x.shape
    _, _, N = wg.shape
    n_n = pl.cdiv(N, tn)
    n_m = M // tm

    def x_map(m, n, te, ns, src):
        return (src[m], 0)

    def w_map(m, n, te, ns, src):
        return (te[m], 0, jnp.where(ns[m] > 0, n, n_n - 1))

    grid_spec = pltpu.PrefetchScalarGridSpec(
        num_scalar_prefetch=3, grid=(n_m, n_n),
        in_specs=[pl.BlockSpec((tm, K), x_map),
                  pl.BlockSpec((None, K, tn), w_map),
                  pl.BlockSpec((None, K, tn), w_map)],
        out_specs=pl.BlockSpec((tm, tn), lambda m, n, te, ns, src: (m, n)))
    return pl.pallas_call(
        functools.partial(_swiglu_kernel, nsub=tm // sub, sub=sub),
        grid_spec=grid_spec, out_shape=jax.ShapeDtypeStruct((M, N), BF16),
        compiler_params=_params(("parallel", "arbitrary")), name="swiglu_up")(*tiles, x, wg, wu)


def _down_dense_kernel(xm_ref, xr_ref, wm_ref, wr_ref, o_ref, acc_ref, *, n_main):
    k = pl.program_id(2)

    @pl.when(k == 0)
    def _():
        acc_ref[...] = jnp.dot(xm_ref[...], wm_ref[...].astype(BF16), preferred_element_type=F32)

    @pl.when((k > 0) & (k < n_main))
    def _():
        acc_ref[...] += jnp.dot(xm_ref[...], wm_ref[...].astype(BF16), preferred_element_type=F32)

    @pl.when(k == n_main)
    def _():
        o_ref[...] = (acc_ref[...] + jnp.dot(xr_ref[...], wr_ref[...].astype(BF16),
                                             preferred_element_type=F32)).astype(o_ref.dtype)


def _down_dense(x, w3, g, tm, tn, tk, tk_rem, out_dtype):
    M, K = x.shape
    _, _, N = w3.shape
    n_main = (K - tk_rem) // tk
    assert n_main * tk + tk_rem == K and (K - tk_rem) % tk_rem == 0
    rem_idx = (K - tk_rem) // tk_rem

    def km(k):
        return jnp.minimum(k, n_main - 1)

    return pl.pallas_call(
        functools.partial(_down_dense_kernel, n_main=n_main),
        grid=(M // tm, N // tn, n_main + 1),
        in_specs=[pl.BlockSpec((tm, tk), lambda m, n, k: (m, km(k))),
                  pl.BlockSpec((tm, tk_rem), lambda m, n, k: (m, rem_idx)),
                  pl.BlockSpec((None, tk, tn), lambda m, n, k: (g, km(k), n)),
                  pl.BlockSpec((None, tk_rem, tn), lambda m, n, k: (g, rem_idx, n))],
        out_specs=pl.BlockSpec((tm, tn), lambda m, n, k: (m, n)),
        out_shape=jax.ShapeDtypeStruct((M, N), out_dtype),
        scratch_shapes=[pltpu.VMEM((tm, tn), F32)],
        compiler_params=_params(("parallel", "arbitrary", "arbitrary")), name="down_dense")(x, x, w3, w3)


def _down_grouped_kernel(te_ref, ns_ref, src_ref, x_ref, w_ref, o_ref, *, nsub, sub):
    ns = ns_ref[pl.program_id(0)]
    for v in range(nsub + 1):
        @pl.when(ns == v)
        def _(v=v):
            rows = v * sub
            if v > 0:
                o_ref[:rows, :] = jnp.dot(x_ref[:rows, :], w_ref[...].astype(BF16),
                                          preferred_element_type=F32).astype(o_ref.dtype)
            if v < nsub:
                o_ref[rows:, :] = jnp.zeros((nsub * sub - rows, o_ref.shape[1]), o_ref.dtype)


def _down_grouped(x, w3, tiles, tm, tn, sub, out_dtype):
    M, K = x.shape
    _, _, N = w3.shape
    n_n = N // tn

    def w_map(m, n, te, ns, src):
        return (te[m], 0, jnp.where(ns[m] > 0, n, n_n - 1))

    grid_spec = pltpu.PrefetchScalarGridSpec(
        num_scalar_prefetch=3, grid=(M // tm, n_n),
        in_specs=[pl.BlockSpec((tm, K), lambda m, n, te, ns, src: (src[m], 0)),
                  pl.BlockSpec((None, K, tn), w_map)],
        out_specs=pl.BlockSpec((tm, tn), lambda m, n, te, ns, src: (m, n)))
    return pl.pallas_call(
        functools.partial(_down_grouped_kernel, nsub=tm // sub, sub=sub),
        grid_spec=grid_spec, out_shape=jax.ShapeDtypeStruct((M, N), out_dtype),
        compiler_params=_params(("parallel", "arbitrary")), name="down_grouped")(*tiles, x, w3)


def _rope(x, c, sa, sb):
    return x * c + pltpu.roll(x, LANE - MLA_ROPE // 2, 1) * sa + pltpu.roll(x, MLA_ROPE // 2, 1) * sb


def _mla_project_kernel(cq_ref, ckv_ref, kpe_ref, qg_ref, kvg_ref, wq_ref, wkv_ref, c_ref, sa_ref, sb_ref,
                        q_ref, k_ref, v_ref, *, scale):
    c, sa, sb = c_ref[...], sa_ref[...], sb_ref[...]
    xq = _rms(cq_ref[...].astype(F32), qg_ref[...]).astype(BF16)
    rq = jnp.dot(xq, wq_ref[...], preferred_element_type=F32)
    xkv = _rms(ckv_ref[...].astype(F32), kvg_ref[...]).astype(BF16)
    rkv = jnp.dot(xkv, wkv_ref[...], preferred_element_type=F32)
    kpe = _rope(kpe_ref[...].astype(F32), c, sa, sb).astype(k_ref.dtype)
    for h in range(MLA_HEADS):
        lo = h * MLA_HEAD_PAD
        mid = lo + MLA_NOPE
        hi = lo + MLA_HEAD_PAD
        q_ref[h, :, :MLA_NOPE] = (rq[:, lo:mid] * scale).astype(q_ref.dtype)
        q_ref[h, :, MLA_NOPE:] = (_rope(rq[:, mid:hi], c, sa, sb) * scale).astype(q_ref.dtype)
        k_ref[h, :, :MLA_NOPE] = rkv[:, lo:mid].astype(k_ref.dtype)
        k_ref[h, :, MLA_NOPE:] = kpe
        v_ref[h, :, :] = rkv[:, mid:hi].astype(v_ref.dtype)


def _mla_project(u, B, S, q_g, wq, kv_g, wkv, tabs, tm=512):
    T = B * S
    n_s = S // tm
    H = MLA_HEADS
    tab_spec = pl.BlockSpec((tm, LANE), lambda m: (m % n_s, 0))

    def const(shape):
        return pl.BlockSpec(shape, lambda m: (0,) * len(shape))

    def head_spec(width):
        return pl.BlockSpec((None, H, tm, width), lambda m: (m // n_s, 0, m % n_s, 0))

    return pl.pallas_call(
        functools.partial(_mla_project_kernel, scale=MLA_QK ** -0.5 * LOG2_E),
        grid=(T // tm,),
        in_specs=[pl.BlockSpec((tm, MLA_Q_LORA), lambda m: (m, U_CQ // MLA_Q_LORA)),
                  pl.BlockSpec((tm, MLA_KV_LORA), lambda m: (m, U_CKV // MLA_KV_LORA)),
                  pl.BlockSpec((tm, LANE), lambda m: (m, U_KPE // LANE)),
                  const((1, MLA_Q_LORA)), const((1, MLA_KV_LORA)),
                  const((MLA_Q_LORA, H * MLA_HEAD_PAD)), const((MLA_KV_LORA, H * MLA_HEAD_PAD)),
                  tab_spec, tab_spec, tab_spec],
        out_specs=[head_spec(MLA_HEAD_PAD), head_spec(MLA_HEAD_PAD), head_spec(MLA_V)],
        out_shape=[jax.ShapeDtypeStruct((B, H, S, MLA_HEAD_PAD), BF16),
                   jax.ShapeDtypeStruct((B, H, S, MLA_HEAD_PAD), BF16),
                   jax.ShapeDtypeStruct((B, H, S, MLA_V), BF16)],
        compiler_params=_params(("parallel",)), name="mla_project")(u, u, u, q_g, kv_g, wq, wkv, *tabs)


def _flash_kernel(q_ref, k_ref, v_ref, o_ref, *, tq, tk, hp):
    qi = pl.program_id(2)

    nk = tq // tk

    def step(j, carry, diag):
        start = pl.multiple_of(j * tk, tk)
        out = []
        for hh in range(hp):
            m, l, acc = carry[hh]
            s = lax.dot_general(q_ref[hh], k_ref[hh, pl.ds(start, tk), :], (((1,), (1,)), ((), ())),
                                preferred_element_type=F32)
            if diag is not None:
                row = lax.broadcasted_iota(jnp.int32, (tq, tk), 0)
                col = lax.broadcasted_iota(jnp.int32, (tq, tk), 1) + diag * tk
                s = jnp.where(row >= col, s, NEG_INF)
            m_new = jnp.maximum(m, jnp.max(s, axis=-1, keepdims=True))
            alpha = jnp.exp2(m - m_new)
            p = jnp.exp2(s - m_new)
            l = alpha * l + jnp.sum(p, axis=-1, keepdims=True)
            acc = alpha * acc + jnp.dot(p.astype(BF16), v_ref[hh, pl.ds(start, tk), :],
                                        preferred_element_type=F32)
            out.append((m_new, l, acc))
        return tuple(out)

    carry = tuple((jnp.full((tq, 1), NEG_INF, F32), jnp.zeros((tq, 1), F32), jnp.zeros((tq, MLA_V), F32))
                  for _ in range(hp))
    def full_tile(t, c):
        for d in range(nk):
            c = step(t * nk + d, c, None)
        return c

    carry = lax.fori_loop(0, qi, full_tile, carry)
    for d in range(nk):
        carry = step(qi * nk + d, carry, d)
    for hh in range(hp):
        _, l, acc = carry[hh]
        o_ref[:, hh * MLA_V:(hh + 1) * MLA_V] = (acc / l).astype(o_ref.dtype)


def _mla_attention(q, k, v, tq=1024, tk=512, hp=2):
    B, H, S, _ = q.shape
    return pl.pallas_call(
        functools.partial(_flash_kernel, tq=tq, tk=tk, hp=hp),
        grid=(B, H // hp, S // tq),
        in_specs=[pl.BlockSpec((None, hp, tq, MLA_HEAD_PAD), lambda b, h, i: (b, h, i, 0)),
                  pl.BlockSpec((None, hp, S, MLA_HEAD_PAD), lambda b, h, i: (b, h, 0, 0)),
                  pl.BlockSpec((None, hp, S, MLA_V), lambda b, h, i: (b, h, 0, 0))],
        out_specs=pl.BlockSpec((None, tq, hp * MLA_V), lambda b, h, i: (b, i, h)),
        out_shape=jax.ShapeDtypeStruct((B, S, H * MLA_V), BF16),
        compiler_params=_params(("parallel", "parallel", "arbitrary")), name="mla_attention")(q, k, v)


def _pool_kernel(u_ref, w_ref, sc_ref, gn_ref, o_ref, buf_ref, *, ts):
    s = pl.program_id(1)

    @pl.when(s == 0)
    def _():
        buf_ref[0:POOL_HALO, :] = jnp.zeros((POOL_HALO, GROUP_WIDTH), F32)

    @pl.when(s > 0)
    def _():
        buf_ref[0:POOL_HALO, :] = buf_ref[ts:ts + POOL_HALO, :]

    buf_ref[POOL_HALO:POOL_HALO + ts, :] = u_ref[...].astype(F32)
    pos = s * ts + lax.broadcasted_iota(jnp.int32, (ts, 1), 0)
    ys = []
    ss = jnp.zeros((ts, 1), F32)
    for gi, w in enumerate(POOL_WINDOWS):
        lanes = slice(gi * POOL_GROUP, (gi + 1) * POOL_GROUP)
        ext = buf_ref[:, lanes]
        win = ext
        span = 1
        while span < w:
            win = win + pltpu.roll(win, span, 0)
            span *= 2
        cur = ext[POOL_HALO:, :]
        win = win[POOL_HALO:, :]
        count = jnp.minimum(pos + 1, w).astype(F32)
        d = win / count - cur
        y = jnp.dot(d.astype(BF16), w_ref[gi], preferred_element_type=F32) * sc_ref[:, lanes]
        ss = ss + jnp.sum(y * y, axis=-1, keepdims=True)
        ys.append(y)
    r = lax.rsqrt(ss / GROUP_WIDTH + EPS)
    for gi, y in enumerate(ys):
        lanes = slice(gi * POOL_GROUP, (gi + 1) * POOL_GROUP)
        o_ref[:, lanes] = (y * r * gn_ref[:, lanes]).astype(o_ref.dtype)


def _pool(u3, w_pool, pool_scale, gn_g, ts=512):
    B, S, _ = u3.shape
    return pl.pallas_call(
        functools.partial(_pool_kernel, ts=ts),
        grid=(B, S // ts),
        in_specs=[pl.BlockSpec((None, ts, GROUP_WIDTH), lambda b, s: (b, s, U_POOL // GROUP_WIDTH)),
                  pl.BlockSpec((len(POOL_WINDOWS), POOL_GROUP, POOL_GROUP), lambda b, s: (0, 0, 0)),
                  pl.BlockSpec((1, GROUP_WIDTH), lambda b, s: (0, 0)),
                  pl.BlockSpec((1, GROUP_WIDTH), lambda b, s: (0, 0))],
        out_specs=pl.BlockSpec((None, ts, GROUP_WIDTH), lambda b, s: (b, s, 0)),
        out_shape=jax.ShapeDtypeStruct((B, S, GROUP_WIDTH), BF16),
        scratch_shapes=[pltpu.VMEM((POOL_HALO + ts, GROUP_WIDTH), F32)],
        compiler_params=_params(("parallel", "arbitrary")), name="pool_mixer")(u3, w_pool, pool_scale, gn_g)


def _conv_kernel(a_ref, gate_ref, wdw_ref, bdw_ref, lng_ref, lnb_ref, wpw_ref, gn_ref, o_ref,
                 buf_ref, zc_ref, *, ts, rc):
    s = pl.program_id(1)

    @pl.when(s == 0)
    def _():
        buf_ref[0:CONV_HALO, :] = jnp.zeros((CONV_HALO, GROUP_WIDTH), F32)

    @pl.when(s > 0)
    def _():
        buf_ref[0:CONV_HALO, :] = buf_ref[ts:ts + CONV_HALO, :]

    a = a_ref[...].astype(F32)
    gate = gate_ref[...].astype(F32)
    buf_ref[CONV_HALO:CONV_HALO + ts, :] = a * jax.nn.sigmoid(gate)
    first = CONV_HALO - (CONV_WIDTH - 1)

    sub = 8
    n_win = rc + CONV_HALO

    def lane_block(c, _):
        lanes = pl.ds(pl.multiple_of(c * LANE, LANE), LANE)
        for r0 in range(0, ts, rc):
            acc = jnp.broadcast_to(bdw_ref[:, lanes], (rc, LANE))
            win = buf_ref[r0:r0 + n_win, lanes]
            for b in range(sub):
                taps = [j for j in range(CONV_WIDTH) if (first + j) % sub == b]
                if b == 0:
                    for j in taps:
                        acc = acc + buf_ref[r0 + first + j:r0 + first + j + rc, lanes] * wdw_ref[j:j + 1, lanes]
                else:
                    shifted = pltpu.roll(win, n_win - b, 0)
                    for j in taps:
                        a = (first + j - b)
                        acc = acc + shifted[a:a + rc, :] * wdw_ref[j:j + 1, lanes]
            zc_ref[r0:r0 + rc, lanes] = acc
        return 0

    lax.fori_loop(0, GROUP_WIDTH // LANE, lane_block, 0)
    z = zc_ref[...]
    mu = jnp.mean(z, axis=-1, keepdims=True)
    zc = z - mu
    zn = zc * lax.rsqrt(jnp.mean(zc * zc, axis=-1, keepdims=True) + EPS) * lng_ref[...] + lnb_ref[...]
    act = zn * jax.nn.sigmoid(zn)
    y = jnp.dot(act.astype(BF16), wpw_ref[...], preferred_element_type=F32)
    o_ref[...] = _rms(y, gn_ref[...]).astype(o_ref.dtype)


def _conv(u3, w_dw, b_dw, ln_g, ln_b, w_pw, gn_g, ts=512, rc=64):
    B, S, _ = u3.shape
    vec = pl.BlockSpec((1, GROUP_WIDTH), lambda b, s: (0, 0))
    return pl.pallas_call(
        functools.partial(_conv_kernel, ts=ts, rc=rc),
        grid=(B, S // ts),
        in_specs=[pl.BlockSpec((None, ts, GROUP_WIDTH), lambda b, s: (b, s, U_CONV_A // GROUP_WIDTH)),
                  pl.BlockSpec((None, ts, GROUP_WIDTH), lambda b, s: (b, s, U_CONV_G // GROUP_WIDTH)),
                  pl.BlockSpec((CONV_HALO, GROUP_WIDTH), lambda b, s: (0, 0)),
                  vec, vec, vec,
                  pl.BlockSpec((GROUP_WIDTH, GROUP_WIDTH), lambda b, s: (0, 0)),
                  vec],
        out_specs=pl.BlockSpec((None, ts, GROUP_WIDTH), lambda b, s: (b, s, 0)),
        out_shape=jax.ShapeDtypeStruct((B, S, GROUP_WIDTH), BF16),
        scratch_shapes=[pltpu.VMEM((CONV_HALO + ts, GROUP_WIDTH), F32), pltpu.VMEM((ts, GROUP_WIDTH), F32)],
        compiler_params=_params(("parallel", "arbitrary")), name="conv_mixer")(
            u3, u3, w_dw, b_dw, ln_g, ln_b, w_pw, gn_g)


def _swa_kernel(sink_ref, q_ref, kp_ref, kc_ref, vp_ref, vc_ref, gn_ref, o_ref):
    n = pl.program_id(1)
    W = SWA_WINDOW
    dh = SWA_HEAD_DIM
    R = SWA_Q_HEADS // SWA_KV_HEADS
    q = q_ref[...] * (dh ** -0.5)
    k2 = jnp.concatenate([kp_ref[...], kc_ref[...]], axis=0)
    v2 = jnp.concatenate([vp_ref[...], vc_ref[...]], axis=0)
    qi = lax.broadcasted_iota(jnp.int32, (W, 2 * W), 0)
    kj = lax.broadcasted_iota(jnp.int32, (W, 2 * W), 1)
    rel = qi + W - kj
    valid = (rel >= 0) & (rel < W) & (n * W + kj - W >= 0)
    outs = []
    for g in range(SWA_KV_HEADS):
        kg = k2[:, g * dh:(g + 1) * dh]
        vg = v2[:, g * dh:(g + 1) * dh]
        for r in range(R):
            h = g * R + r
            s = lax.dot_general(q[:, h * dh:(h + 1) * dh], kg, (((1,), (1,)), ((), ())),
                                preferred_element_type=F32)
            s = jnp.where(valid, s, NEG_INF)
            sink = sink_ref[h]
            m = jnp.maximum(jnp.max(s, axis=-1, keepdims=True), sink)
            e = jnp.exp(s - m)
            denom = jnp.sum(e, axis=-1, keepdims=True) + jnp.exp(sink - m)
            p = e / denom
            outs.append(jnp.dot(p.astype(BF16), vg, preferred_element_type=F32))
    y = jnp.concatenate(outs, axis=-1)
    o_ref[...] = _rms(y, gn_ref[...]).astype(o_ref.dtype)


def _swa(u3, sinks, gn_g):
    B, S, _ = u3.shape
    W = SWA_WINDOW
    kcol, vcol = U_SWA_K // LANE, U_SWA_V // LANE

    def prev(col):
        return pl.BlockSpec((None, W, LANE), lambda b, n, sk: (b, jnp.maximum(n - 1, 0), col))

    def cur(col):
        return pl.BlockSpec((None, W, LANE), lambda b, n, sk: (b, n, col))

    grid_spec = pltpu.PrefetchScalarGridSpec(
        num_scalar_prefetch=1, grid=(B, S // W),
        in_specs=[pl.BlockSpec((None, W, GROUP_WIDTH), lambda b, n, sk: (b, n, U_SWA_Q // GROUP_WIDTH)),
                  prev(kcol), cur(kcol), prev(vcol), cur(vcol),
                  pl.BlockSpec((1, GROUP_WIDTH), lambda b, n, sk: (0, 0))],
        out_specs=pl.BlockSpec((None, W, GROUP_WIDTH), lambda b, n, sk: (b, n, 0)))
    return pl.pallas_call(
        _swa_kernel, grid_spec=grid_spec,
        out_shape=jax.ShapeDtypeStruct((B, S, GROUP_WIDTH), BF16),
        compiler_params=_params(("parallel", "arbitrary")), name="swa_mixer")(
            sinks, u3, u3, u3, u3, u3, gn_g)


def _router_kernel(x_ref, g_ref, w_ref, o_ref, cnt_ref, carry_ref, *, tm):
    @pl.when(pl.program_id(0) == 0)
    def _():
        carry_ref[...] = jnp.zeros_like(carry_ref)

    h = _rms(x_ref[...], g_ref[...])
    logits = jnp.dot(h.astype(BF16), w_ref[...], preferred_element_type=F32)
    lane = lax.broadcasted_iota(jnp.int32, (tm, LANE), 1).astype(F32)
    logits = jnp.where(lane < N_EXPERTS, logits, -jnp.inf)
    m1 = jnp.max(logits, axis=-1, keepdims=True)
    i1 = jnp.min(jnp.where(logits == m1, lane, float(LANE)), axis=-1, keepdims=True)
    rest = jnp.where(lane == i1, -jnp.inf, logits)
    m2 = jnp.max(rest, axis=-1, keepdims=True)
    i2 = jnp.min(jnp.where(rest == m2, lane, float(LANE)), axis=-1, keepdims=True)
    e2 = jnp.exp(m2 - m1)
    w1 = 1.0 / (1.0 + e2)
    w2 = e2 / (1.0 + e2)
    oh1 = (lane == i1).astype(F32)
    oh2 = (lane == i2).astype(F32)
    cnt = oh1 + oh2
    row = lax.broadcasted_iota(jnp.int32, (tm, tm), 0)
    col = lax.broadcasted_iota(jnp.int32, (tm, tm), 1)
    before = (row > col).astype(BF16)
    pre = jnp.dot(before, cnt.astype(BF16), preferred_element_type=F32) + carry_ref[0:1, :]
    r1 = jnp.sum(oh1 * pre, axis=-1, keepdims=True)
    r2 = jnp.sum(oh2 * pre, axis=-1, keepdims=True)
    carry_ref[0:1, :] = carry_ref[0:1, :] + jnp.sum(cnt, axis=0, keepdims=True)
    cols = (i1, i2, r1, r2, w1, w2)
    out = jnp.zeros((tm, LANE), F32)
    for ci, val in enumerate(cols):
        out = jnp.where(lane == ci, val, out)
    o_ref[...] = out
    cnt_ref[...] = jnp.broadcast_to(carry_ref[0:1, :], cnt_ref.shape)


def _router(x, g, w_router_pad, tm=256):
    T, D = x.shape
    return pl.pallas_call(
        functools.partial(_router_kernel, tm=tm),
        grid=(T // tm,),
        in_specs=[pl.BlockSpec((tm, D), lambda i: (i, 0)),
                  pl.BlockSpec((1, D), lambda i: (0, 0)),
                  pl.BlockSpec((D, LANE), lambda i: (0, 0))],
        out_specs=[pl.BlockSpec((tm, LANE), lambda i: (i, 0)),
                   pl.BlockSpec((8, LANE), lambda i: (0, 0))],
        out_shape=[jax.ShapeDtypeStruct((T, LANE), F32), jax.ShapeDtypeStruct((8, LANE), F32)],
        scratch_shapes=[pltpu.VMEM((8, LANE), F32)],
        compiler_params=_params(("arbitrary",)), name="router")(x, g, w_router_pad)


def _row_copy(src_hbm, row, dst, slot, sem):
    return pltpu.make_async_copy(src_hbm.at[pl.ds(row, 1), :], dst.at[pl.ds(slot, 1), :], sem)


GATHER_UNROLL = 8
GATHER_SLOTS = 2
NORM_CHUNK = 16


def _dispatch_kernel(tok_ref, ns_ref, x_hbm, g_ref, o_ref, buf_ref, sem, *, rows, per_tile, n_blocks):
    i = pl.program_id(0)

    def live(j):
        return j % per_tile < ns_ref[j // per_tile]

    def gather(j, slot, wait):
        def body(r, _):
            cp = _row_copy(x_hbm, tok_ref[j * rows + r], buf_ref.at[slot], r, sem.at[slot])
            cp.wait() if wait else cp.start()
            return 0

        lax.fori_loop(0, rows, body, 0, unroll=GATHER_UNROLL)

    @pl.when((i == 0) & live(0))
    def _():
        gather(0, 0, False)

    nxt = jnp.minimum(i + 1, n_blocks - 1)

    @pl.when((i + 1 < n_blocks) & live(nxt))
    def _():
        gather(nxt, (i + 1) % GATHER_SLOTS, False)

    @pl.when(jnp.logical_not(live(i)))
    def _():
        o_ref[...] = jnp.zeros(o_ref.shape, o_ref.dtype)

    @pl.when(live(i))
    def _():
        slot = i % GATHER_SLOTS
        gather(i, slot, True)

        def norm_rows(c, _):
            r0 = pl.multiple_of(c * NORM_CHUNK, NORM_CHUNK)
            o_ref[pl.ds(r0, NORM_CHUNK), :] = _rms(buf_ref[slot, pl.ds(r0, NORM_CHUNK), :],
                                                   g_ref[...]).astype(o_ref.dtype)
            return 0

        lax.fori_loop(0, rows // NORM_CHUNK, norm_rows, 0, unroll=True)


def _dispatch(x, g, row_token, tile_nsub, n_rows):
    T, D = x.shape
    rows = MOE_SUB
    n_blocks = n_rows // rows
    grid_spec = pltpu.PrefetchScalarGridSpec(
        num_scalar_prefetch=2, grid=(n_blocks,),
        in_specs=[pl.BlockSpec(memory_space=pl.ANY),
                  pl.BlockSpec((1, D), lambda i, tok, ns: (0, 0))],
        out_specs=pl.BlockSpec((rows, D), lambda i, tok, ns: (i, 0)),
        scratch_shapes=[pltpu.VMEM((GATHER_SLOTS, rows, D), F32), pltpu.SemaphoreType.DMA((GATHER_SLOTS,))])
    return pl.pallas_call(
        functools.partial(_dispatch_kernel, rows=rows, per_tile=MOE_TILE // MOE_SUB, n_blocks=n_blocks),
        grid_spec=grid_spec, out_shape=jax.ShapeDtypeStruct((n_rows, D), BF16),
        compiler_params=_params(("arbitrary",)), name="moe_dispatch")(row_token, tile_nsub, x, g)


def _combine_kernel(p0_ref, p1_ref, x_ref, gate_ref, eo_hbm, g_ref, o_ref, a_ref, b_ref, sem, *, rows, n_blocks):
    i = pl.program_id(0)

    def gather(j, slot, wait):
        def body(r, _):
            for p_ref, dst in ((p0_ref, a_ref), (p1_ref, b_ref)):
                cp = _row_copy(eo_hbm, p_ref[j * rows + r], dst.at[slot], r, sem.at[slot])
                cp.wait() if wait else cp.start()
            return 0

        lax.fori_loop(0, rows, body, 0, unroll=GATHER_UNROLL // 2)

    @pl.when(i == 0)
    def _():
        gather(0, 0, False)

    @pl.when(i + 1 < n_blocks)
    def _():
        gather(i + 1, (i + 1) % GATHER_SLOTS, False)

    slot = i % GATHER_SLOTS
    gather(i, slot, True)

    def mix_rows(c, _):
        rs = pl.ds(pl.multiple_of(c * NORM_CHUNK, NORM_CHUNK), NORM_CHUNK)
        gate = gate_ref[rs, :]
        y = x_ref[rs, :] + (gate[:, 0:1] * a_ref[slot, rs, :] + gate[:, 1:2] * b_ref[slot, rs, :])
        o_ref[rs, :] = _rms(y, g_ref[...]).astype(o_ref.dtype)
        return 0

    lax.fori_loop(0, rows // NORM_CHUNK, mix_rows, 0, unroll=True)


def _combine(x, eo, pos0, pos1, gate, g, rows=128):
    T, D = x.shape
    n_blocks = T // rows
    grid_spec = pltpu.PrefetchScalarGridSpec(
        num_scalar_prefetch=2, grid=(n_blocks,),
        in_specs=[pl.BlockSpec((rows, D), lambda i, p0, p1: (i, 0)),
                  pl.BlockSpec((rows, 2), lambda i, p0, p1: (i, 0)),
                  pl.BlockSpec(memory_space=pl.ANY),
                  pl.BlockSpec((1, D), lambda i, p0, p1: (0, 0))],
        out_specs=pl.BlockSpec((rows, D), lambda i, p0, p1: (i, 0)),
        scratch_shapes=[pltpu.VMEM((GATHER_SLOTS, rows, D), F32), pltpu.VMEM((GATHER_SLOTS, rows, D), F32),
                        pltpu.SemaphoreType.DMA((GATHER_SLOTS,))])
    return pl.pallas_call(
        functools.partial(_combine_kernel, rows=rows, n_blocks=n_blocks),
        grid_spec=grid_spec, out_shape=jax.ShapeDtypeStruct((T, D), F32),
        compiler_params=_params(("arbitrary",)), name="moe_combine")(pos0, pos1, x, gate, eo, g)


W_IN_SEGMENTS = (
    (1600, U_POOL, 1024),
    (2624, U_CONV_A, 2048),
    (4672, U_SWA_Q, 1024),
    (0, U_CQ, 1536),
    (1536, U_KPE, 64),
    (5696, U_SWA_K, 256),
)


def _pack_kernel(w_ref, o_ref):
    o_ref[...] = jnp.zeros(o_ref.shape, o_ref.dtype)
    for src, dst, width in W_IN_SEGMENTS:
        o_ref[dst:dst + width, :] = w_ref[src:src + width, :].astype(o_ref.dtype)


def _pack_w_in(w_in_t, tc=256):
    L, W, D = w_in_t.shape
    return pl.pallas_call(
        _pack_kernel, grid=(L, D // tc),
        in_specs=[pl.BlockSpec((None, W, tc), lambda l, c: (l, 0, c))],
        out_specs=pl.BlockSpec((None, U_WIDTH, tc), lambda l, c: (l, 0, c)),
        out_shape=jax.ShapeDtypeStruct((L, U_WIDTH, D), BF16),
        compiler_params=_params(("parallel", "parallel")), name="pack_w_in")(w_in_t)


def _rope_tables(S):
    inv = 1.0 / (ROPE_THETA ** (jnp.arange(0, MLA_ROPE, 2, dtype=F32) / MLA_ROPE))
    ang = jnp.arange(S, dtype=F32)[:, None] * inv[None, :]
    cos, sin = jnp.cos(ang), jnp.sin(ang)
    z32 = jnp.zeros_like(cos)
    z64 = jnp.zeros((S, 64), F32)
    c = jnp.concatenate([cos, cos, z64], axis=-1)
    sa = jnp.concatenate([-sin, z32, z64], axis=-1)
    sb = jnp.concatenate([z32, sin, z64], axis=-1)
    return c, sa, sb


def _moe_plan(route, counts, T, n_tiles):
    E = N_EXPERTS
    expert = route[:, 0:2].astype(jnp.int32)
    rank = route[:, 2:4].astype(jnp.int32)
    gate = route[:, 4:6]
    counts = counts.astype(jnp.int32)
    tiles_per = (counts + MOE_TILE - 1) // MOE_TILE
    tile_end = jnp.cumsum(tiles_per)
    tile_start = tile_end - tiles_per
    used = tile_end[E - 1]
    even = (counts + jnp.maximum(tiles_per, 1) - 1) // jnp.maximum(tiles_per, 1)
    per = (even + MOE_SUB - 1) // MOE_SUB * MOE_SUB
    per = jnp.maximum(per, MOE_SUB)
    per_tok = per[expert]
    sub_tile = sum((rank >= m * per_tok).astype(jnp.int32) for m in range(1, pl.cdiv(T, MOE_TILE)))
    pos = (tile_start[expert] + sub_tile) * MOE_TILE + (rank - sub_tile * per_tok)
    n_rows = n_tiles * MOE_TILE
    flat = pos.reshape(-1)
    token = jnp.repeat(jnp.arange(T, dtype=jnp.int32), 2)
    row_token = jnp.zeros((n_rows,), jnp.int32).at[flat].set(token, unique_indices=True)
    t = jnp.arange(n_tiles, dtype=jnp.int32)
    te = jnp.minimum(jnp.sum((t[:, None] >= tile_end[None, :]).astype(jnp.int32), axis=1), E - 1)
    last = jnp.maximum(used - 1, 0)
    te = jnp.where(t < used, te, te[last])
    live_rows = jnp.clip(counts[te] - (t - tile_start[te]) * per[te], 0, per[te])
    nsub = jnp.where(t < used, (live_rows + MOE_SUB - 1) // MOE_SUB, 0).astype(jnp.int32)
    src = jnp.minimum(t, last)
    return pos[:, 0], pos[:, 1], gate, row_token, (te.astype(jnp.int32), nsub, src)


def kernel(x, attn_norm_g, w_in, mla_q_norm_g, mla_w_q_up, mla_kv_norm_g, mla_w_kv_up, pool_w, pool_scale,
           conv_w_dw, conv_b_dw, conv_ln_g, conv_ln_b, conv_w_pw, swa_sinks, group_out_g, w_out, ffn_norm_g,
           dense_w_gate, dense_w_up, dense_w_down, moe_w_router, moe_w_gate, moe_w_up, moe_w_down, final_norm_g):
    B, S, D = x.shape
    T = B * S
    L = w_in.shape[0]
    H = MLA_HEADS
    x = x.reshape(T, D)

    w_in_p = _pack_w_in(jnp.swapaxes(w_in, 1, 2))
    wq = jnp.pad(mla_w_q_up.reshape(L, MLA_Q_LORA, H, MLA_QK), ((0, 0), (0, 0), (0, 0), (0, MLA_HEAD_PAD - MLA_QK)))
    wq = wq.reshape(L, MLA_Q_LORA, H * MLA_HEAD_PAD).astype(BF16)
    wkv = mla_w_kv_up.astype(BF16)
    tabs = _rope_tables(S)
    pool_w_b = pool_w.astype(BF16)
    conv_w_pw_b = conv_w_pw.astype(BF16)
    conv_w_dw_p = jnp.pad(conv_w_dw, ((0, 0), (0, CONV_HALO - CONV_WIDTH), (0, 0)))
    gn = group_out_g.reshape(L, 4, 1, GROUP_WIDTH)
    row = lambda v: v.reshape(1, -1).astype(F32)

    assert L == 2, "layer 0 dense FFN, layer 1 (last) expert FFN"
    n_dense_tiles = T // DENSE_TM
    dense_tiles = (jnp.zeros((n_dense_tiles,), jnp.int32), jnp.ones((n_dense_tiles,), jnp.int32),
                   jnp.arange(n_dense_tiles, dtype=jnp.int32))

    delta = None
    out = None
    for l in range(L):
        if delta is None:
            h = _norm(x, attn_norm_g[l])
        else:
            x, h = _norm(x, attn_norm_g[l], delta=delta, write_sum=True)
        u = _mm_nt(h, w_in_p, l, out_dtype=BF16, **MM_IN)
        u3 = u.reshape(B, S, U_WIDTH)
        q, k, v = _mla_project(u, B, S, row(mla_q_norm_g[l]), wq[l], row(mla_kv_norm_g[l]), wkv[l], tabs)
        y_a = _mla_attention(q, k, v).reshape(T, GROUP_WIDTH)
        y_a = _norm(y_a, gn[l, 0], out_dtype=BF16, tm=1024)
        y_b = _pool(u3, pool_w_b[l], row(pool_scale[l]), gn[l, 1]).reshape(T, GROUP_WIDTH)
        y_c = _conv(u3, conv_w_dw_p[l], row(conv_b_dw[l]), row(conv_ln_g[l]), row(conv_ln_b[l]),
                    conv_w_pw_b[l], gn[l, 2]).reshape(T, GROUP_WIDTH)
        y_d = _swa(u3, swa_sinks[l].astype(F32), gn[l, 3]).reshape(T, GROUP_WIDTH)
        x = _mm([y_a, y_b, y_c, y_d], w_out, l, out_dtype=F32, res=x, **MM_OUT)
        i = l // 2
        if l % 2 == 0:
            h = _norm(x, ffn_norm_g[l])
            act = _swiglu(h, dense_w_gate, dense_w_up, dense_tiles, tm=DENSE_TM, tn=FFN_UP_TN, sub=DENSE_TM)
            delta = _down_dense(act, dense_w_down, i, out_dtype=BF16, **DOWN_DENSE)
        else:
            g = row(ffn_norm_g[l])
            w_r = jnp.pad(moe_w_router[i], ((0, 0), (0, LANE - N_EXPERTS))).astype(BF16)
            route, counts = _router(x, g, w_r)
            n_tiles = pl.cdiv(2 * T, MOE_TILE) + N_EXPERTS
            pos0, pos1, gate, row_token, tiles = _moe_plan(route, counts[0, :N_EXPERTS], T, n_tiles)
            xs = _dispatch(x, g, row_token, tiles[1], n_tiles * MOE_TILE)
            act = _swiglu(xs, moe_w_gate[i], moe_w_up[i], tiles, tm=MOE_TILE, tn=FFN_UP_TN, sub=MOE_SUB)
            eo = _down_grouped(act, moe_w_down[i], tiles, tm=MOE_TILE, tn=DOWN_GROUPED_TN, sub=MOE_SUB,
                               out_dtype=F32)
            out = _combine(x, eo, pos0, pos1, gate, row(final_norm_g))
    return out.reshape(B, S, D)
```

```python
import functools

import jax
import jax.numpy as jnp
from jax import lax
from jax.experimental import pallas as pl
from jax.experimental.pallas import tpu as pltpu

F32 = jnp.float32
BF16 = jnp.bfloat16
EPS = 1e-6
NEG_INF = -1e30
LOG2_E = 1.4426950408889634

GROUP_WIDTH = 1024
MLA_NOPE = 128
MLA_ROPE = 64
MLA_V = 128
MLA_HEADS = 8
MLA_QK = MLA_NOPE + MLA_ROPE
MLA_Q_LORA = 1024
MLA_KV_LORA = 512
MLA_HEAD_PAD = 256
ROPE_THETA = 10000.0
POOL_WINDOWS = (2, 4, 8, 16)
POOL_GROUP = 256
POOL_HALO = 16
CONV_WIDTH = 31
CONV_HALO = 32
SWA_HEAD_DIM = 64
SWA_Q_HEADS = 16
SWA_KV_HEADS = 2
SWA_WINDOW = 128
N_EXPERTS = 8

U_POOL = 0
U_CONV_A = 1024
U_CONV_G = 2048
U_SWA_Q = 3072
U_CQ = 4096
U_CKV = 5120
U_KPE = 5632
U_SWA_K = 5760
U_SWA_V = 5888
U_WIDTH = 6144

LANE = 128
V7X_VMEM_BYTES = 64 * 1024 * 1024
VMEM_LIMIT = V7X_VMEM_BYTES * 7 // 8

MOE_TILE = 1280
MOE_SUB = 128
DENSE_TM = 1024
FFN_UP_TN = 256
MM_IN = dict(tm=1024, tn=1024)
MM_OUT = dict(tm=1024, tn=512)
DOWN_DENSE = dict(tm=2048, tn=1024, tk=1536, tk_rem=256)
DOWN_GROUPED_TN = 256


def _params(sem, vmem=VMEM_LIMIT):
    return pltpu.CompilerParams(dimension_semantics=sem, vmem_limit_bytes=vmem)


def _rms(x, g):
    return x * lax.rsqrt(jnp.mean(x * x, axis=-1, keepdims=True) + EPS) * g


def _norm_kernel(*refs, has_delta, write_sum, out_dtype):
    it = iter(refs)
    x_ref = next(it)
    d_ref = next(it) if has_delta else None
    g_ref = next(it)
    s_ref = next(it) if write_sum else None
    o_ref = next(it)
    x = x_ref[...]
    if has_delta:
        x = x + d_ref[...].astype(F32)
    if write_sum:
        s_ref[...] = x
    o_ref[...] = _rms(x, g_ref[...]).astype(out_dtype)


def _norm(x, g, delta=None, write_sum=False, out_dtype=BF16, tm=256):
    T, D = x.shape
    row = pl.BlockSpec((tm, D), lambda i: (i, 0))
    in_specs = [row] + ([row] if delta is not None else []) + [pl.BlockSpec((1, D), lambda i: (0, 0))]
    args = [x] + ([delta] if delta is not None else []) + [g.reshape(1, D).astype(F32)]
    out_shape = [jax.ShapeDtypeStruct((T, D), out_dtype)]
    out_specs = [row]
    if write_sum:
        out_shape = [jax.ShapeDtypeStruct((T, D), F32)] + out_shape
        out_specs = [row] + out_specs
    res = pl.pallas_call(
        functools.partial(_norm_kernel, has_delta=delta is not None, write_sum=write_sum, out_dtype=out_dtype),
        grid=(T // tm,), in_specs=in_specs, out_specs=out_specs, out_shape=out_shape,
        compiler_params=_params(("parallel",)), name="norm")(*args)
    return res if write_sum else res[0]


def _mm_kernel(*refs, nx, has_res):
    x_refs = refs[:nx]
    w_ref = refs[nx]
    res_ref = refs[nx + 1] if has_res else None
    o_ref = refs[-1]
    acc = None
    off = 0
    for xr in x_refs:
        kx = xr.shape[1]
        p = jnp.dot(xr[...], w_ref[off:off + kx, :].astype(BF16), preferred_element_type=F32)
        acc = p if acc is None else acc + p
        off += kx
    if has_res:
        acc = acc + res_ref[...]
    o_ref[...] = acc.astype(o_ref.dtype)


def _mm(xs, w3, g, tm, tn, out_dtype, res=None):
    M = xs[0].shape[0]
    _, K, N = w3.shape
    assert sum(x.shape[1] for x in xs) == K
    in_specs = [pl.BlockSpec((tm, x.shape[1]), lambda m, n: (m, 0)) for x in xs]
    in_specs.append(pl.BlockSpec((None, K, tn), lambda m, n: (g, 0, n)))
    args = list(xs) + [w3]
    if res is not None:
        in_specs.append(pl.BlockSpec((tm, tn), lambda m, n: (m, n)))
        args.append(res)
    return pl.pallas_call(
        functools.partial(_mm_kernel, nx=len(xs), has_res=res is not None),
        grid=(M // tm, N // tn), in_specs=in_specs,
        out_specs=pl.BlockSpec((tm, tn), lambda m, n: (m, n)),
        out_shape=jax.ShapeDtypeStruct((M, N), out_dtype),
        compiler_params=_params(("parallel", "arbitrary")), name="mm")(*args)


def _mm_nt_kernel(x_ref, wt_ref, o_ref):
    o_ref[...] = lax.dot_general(x_ref[...], wt_ref[...], (((1,), (1,)), ((), ())),
                                 preferred_element_type=F32).astype(o_ref.dtype)


def _mm_nt(x, wt3, g, tm, tn, out_dtype):
    M, K = x.shape
    _, N, _ = wt3.shape
    return pl.pallas_call(
        _mm_nt_kernel, grid=(M // tm, N // tn),
        in_specs=[pl.BlockSpec((tm, K), lambda m, n: (m, 0)),
                  pl.BlockSpec((None, tn, K), lambda m, n: (g, n, 0))],
        out_specs=pl.BlockSpec((tm, tn), lambda m, n: (m, n)),
        out_shape=jax.ShapeDtypeStruct((M, N), out_dtype),
        compiler_params=_params(("parallel", "arbitrary")), name="mm_nt")(x, wt3)


def _swiglu_kernel(te_ref, ns_ref, src_ref, x_ref, wg_ref, wu_ref, o_ref, *, nsub, sub):
    ns = ns_ref[pl.program_id(0)]
    for v in range(nsub + 1):
        @pl.when(ns == v)
        def _(v=v):
            rows = v * sub
            if v > 0:
                x = x_ref[:rows, :]
                g = jnp.dot(x, wg_ref[...].astype(BF16), preferred_element_type=F32)
                u = jnp.dot(x, wu_ref[...].astype(BF16), preferred_element_type=F32)
                o_ref[:rows, :] = (g * jax.nn.sigmoid(g) * u).astype(o_ref.dtype)
            if v < nsub:
                o_ref[rows:, :] = jnp.zeros((nsub * sub - rows, o_ref.shape[1]), o_ref.dtype)


def _swiglu(x, wg, wu, tiles, tm, tn, sub):
    M, K = x.shape
    _, _, N = wg.shape
    n_n = pl.cdiv(N, tn)
    n_m = M // tm

    def x_map(m, n, te, ns, src):
        return (src[m], 0)

    def w_map(m, n, te, ns, src):
        return (te[m], 0, jnp.where(ns[m] > 0, n, n_n - 1))

    grid_spec = pltpu.PrefetchScalarGridSpec(
        num_scalar_prefetch=3, grid=(n_m, n_n),
        in_specs=[pl.BlockSpec((tm, K), x_map),
                  pl.BlockSpec((None, K, tn), w_map),
                  pl.BlockSpec((None, K, tn), w_map)],
        out_specs=pl.BlockSpec((tm, tn), lambda m, n, te, ns, src: (m, n)))
    return pl.pallas_call(
        functools.partial(_swiglu_kernel, nsub=tm // sub, sub=sub),
        grid_spec=grid_spec, out_shape=jax.ShapeDtypeStruct((M, N), BF16),
        compiler_params=_params(("parallel", "arbitrary")), name="swiglu_up")(*tiles, x, wg, wu)


def _down_dense_kernel(xm_ref, xr_ref, wm_ref, wr_ref, o_ref, acc_ref, *, n_main):
    k = pl.program_id(2)

    @pl.when(k == 0)
    def _():
        acc_ref[...] = jnp.dot(xm_ref[...], wm_ref[...].astype(BF16), preferred_element_type=F32)

    @pl.when((k > 0) & (k < n_main))
    def _():
        acc_ref[...] += jnp.dot(xm_ref[...], wm_ref[...].astype(BF16), preferred_element_type=F32)

    @pl.when(k == n_main)
    def _():
        o_ref[...] = (acc_ref[...] + jnp.dot(xr_ref[...], wr_ref[...].astype(BF16),
                                             preferred_element_type=F32)).astype(o_ref.dtype)


def _down_dense(x, w3, g, tm, tn, tk, tk_rem, out_dtype):
    M, K = x.shape
    _, _, N = w3.shape
    n_main = (K - tk_rem) // tk
    assert n_main * tk + tk_rem == K and (K - tk_rem) % tk_rem == 0
    rem_idx = (K - tk_rem) // tk_rem

    def km(k):
        return jnp.minimum(k, n_main - 1)

    return pl.pallas_call(
        functools.partial(_down_dense_kernel, n_main=n_main),
        grid=(M // tm, N // tn, n_main + 1),
        in_specs=[pl.BlockSpec((tm, tk), lambda m, n, k: (m, km(k))),
                  pl.BlockSpec((tm, tk_rem), lambda m, n, k: (m, rem_idx)),
                  pl.BlockSpec((None, tk, tn), lambda m, n, k: (g, km(k), n)),
                  pl.BlockSpec((None, tk_rem, tn), lambda m, n, k: (g, rem_idx, n))],
        out_specs=pl.BlockSpec((tm, tn), lambda m, n, k: (m, n)),
        out_shape=jax.ShapeDtypeStruct((M, N), out_dtype),
        scratch_shapes=[pltpu.VMEM((tm, tn), F32)],
        compiler_params=_params(("parallel", "arbitrary", "arbitrary")), name="down_dense")(x, x, w3, w3)


def _down_grouped_kernel(te_ref, ns_ref, src_ref, x_ref, w_ref, o_ref, *, nsub, sub):
    ns = ns_ref[pl.program_id(0)]
    for v in range(nsub + 1):
        @pl.when(ns == v)
        def _(v=v):
            rows = v * sub
            if v > 0:
                o_ref[:rows, :] = jnp.dot(x_ref[:rows, :], w_ref[...].astype(BF16),
                                          preferred_element_type=F32).astype(o_ref.dtype)
            if v < nsub:
                o_ref[rows:, :] = jnp.zeros((nsub * sub - rows, o_ref.shape[1]), o_ref.dtype)


def _down_grouped(x, w3, tiles, tm, tn, sub, out_dtype):
    M, K = x.shape
    _, _, N = w3.shape
    n_n = N // tn

    def w_map(m, n, te, ns, src):
        return (te[m], 0, jnp.where(ns[m] > 0, n, n_n - 1))

    grid_spec = pltpu.PrefetchScalarGridSpec(
        num_scalar_prefetch=3, grid=(M // tm, n_n),
        in_specs=[pl.BlockSpec((tm, K), lambda m, n, te, ns, src: (src[m], 0)),
                  pl.BlockSpec((None, K, tn), w_map)],
        out_specs=pl.BlockSpec((tm, tn), lambda m, n, te, ns, src: (m, n)))
    return pl.pallas_call(
        functools.partial(_down_grouped_kernel, nsub=tm // sub, sub=sub),
        grid_spec=grid_spec, out_shape=jax.ShapeDtypeStruct((M, N), out_dtype),
        compiler_params=_params(("parallel", "arbitrary")), name="down_grouped")(*tiles, x, w3)


def _rope(x, c, sa, sb):
    return x * c + pltpu.roll(x, LANE - MLA_ROPE // 2, 1) * sa + pltpu.roll(x, MLA_ROPE // 2, 1) * sb


def _mla_project_kernel(cq_ref, ckv_ref, kpe_ref, qg_ref, kvg_ref, wq_ref, wkv_ref, c_ref, sa_ref, sb_ref,
                        q_ref, k_ref, v_ref, *, scale):
    c, sa, sb = c_ref[...], sa_ref[...], sb_ref[...]
    xq = _rms(cq_ref[...].astype(F32), qg_ref[...]).astype(BF16)
    rq = jnp.dot(xq, wq_ref[...], preferred_element_type=F32)
    xkv = _rms(ckv_ref[...].astype(F32), kvg_ref[...]).astype(BF16)
    rkv = jnp.dot(xkv, wkv_ref[...], preferred_element_type=F32)
    kpe = _rope(kpe_ref[...].astype(F32), c, sa, sb).astype(k_ref.dtype)
    for h in range(MLA_HEADS):
        lo = h * MLA_HEAD_PAD
        mid = lo + MLA_NOPE
        hi = lo + MLA_HEAD_PAD
        q_ref[h, :, :MLA_NOPE] = (rq[:, lo:mid] * scale).astype(q_ref.dtype)
        q_ref[h, :, MLA_NOPE:] = (_rope(rq[:, mid:hi], c, sa, sb) * scale).astype(q_ref.dtype)
        k_ref[h, :, :MLA_NOPE] = rkv[:, lo:mid].astype(k_ref.dtype)
        k_ref[h, :, MLA_NOPE:] = kpe
        v_ref[h, :, :] = rkv[:, mid:hi].astype(v_ref.dtype)


def _mla_project(u, B, S, q_g, wq, kv_g, wkv, tabs, tm=512):
    T = B * S
    n_s = S // tm
    H = MLA_HEADS
    tab_spec = pl.BlockSpec((tm, LANE), lambda m: (m % n_s, 0))

    def const(shape):
        return pl.BlockSpec(shape, lambda m: (0,) * len(shape))

    def head_spec(width):
        return pl.BlockSpec((None, H, tm, width), lambda m: (m // n_s, 0, m % n_s, 0))

    return pl.pallas_call(
        functools.partial(_mla_project_kernel, scale=MLA_QK ** -0.5 * LOG2_E),
        grid=(T // tm,),
        in_specs=[pl.BlockSpec((tm, MLA_Q_LORA), lambda m: (m, U_CQ // MLA_Q_LORA)),
                  pl.BlockSpec((tm, MLA_KV_LORA), lambda m: (m, U_CKV // MLA_KV_LORA)),
                  pl.BlockSpec((tm, LANE), lambda m: (m, U_KPE // LANE)),
                  const((1, MLA_Q_LORA)), const((1, MLA_KV_LORA)),
                  const((MLA_Q_LORA, H * MLA_HEAD_PAD)), const((MLA_KV_LORA, H * MLA_HEAD_PAD)),
                  tab_spec, tab_spec, tab_spec],
        out_specs=[head_spec(MLA_HEAD_PAD), head_spec(MLA_HEAD_PAD), head_spec(MLA_V)],
        out_shape=[jax.ShapeDtypeStruct((B, H, S, MLA_HEAD_PAD), BF16),
                   jax.ShapeDtypeStruct((B, H, S, MLA_HEAD_PAD), BF16),
                   jax.ShapeDtypeStruct((B, H, S, MLA_V), BF16)],
        compiler_params=_params(("parallel",)), name="mla_project")(u, u, u, q_g, kv_g, wq, wkv, *tabs)


def _flash_kernel(q_ref, k_ref, v_ref, o_ref, *, tq, tk, hp):
    qi = pl.program_id(2)

    nk = tq // tk

    def step(j, carry, diag):
        start = pl.multiple_of(j * tk, tk)
        out = []
        for hh in range(hp):
            m, l, acc = carry[hh]
            s = lax.dot_general(q_ref[hh], k_ref[hh, pl.ds(start, tk), :], (((1,), (1,)), ((), ())),
                                preferred_element_type=F32)
            if diag is not None:
                row = lax.broadcasted_iota(jnp.int32, (tq, tk), 0)
                col = lax.broadcasted_iota(jnp.int32, (tq, tk), 1) + diag * tk
                s = jnp.where(row >= col, s, NEG_INF)
            m_new = jnp.maximum(m, jnp.max(s, axis=-1, keepdims=True))
            alpha = jnp.exp2(m - m_new)
            p = jnp.exp2(s - m_new)
            l = alpha * l + jnp.sum(p, axis=-1, keepdims=True)
            acc = alpha * acc + jnp.dot(p.astype(BF16), v_ref[hh, pl.ds(start, tk), :],
                                        preferred_element_type=F32)
            out.append((m_new, l, acc))
        return tuple(out)

    carry = tuple((jnp.full((tq, 1), NEG_INF, F32), jnp.zeros((tq, 1), F32), jnp.zeros((tq, MLA_V), F32))
                  for _ in range(hp))
    def full_tile(t, c):
        for d in range(nk):
            c = step(t * nk + d, c, None)
        return c

    carry = lax.fori_loop(0, qi, full_tile, carry)
    for d in range(nk):
        carry = step(qi * nk + d, carry, d)
    for hh in range(hp):
        _, l, acc = carry[hh]
        o_ref[:, hh * MLA_V:(hh + 1) * MLA_V] = (acc / l).astype(o_ref.dtype)


def _mla_attention(q, k, v, tq=1024, tk=512, hp=2):
    B, H, S, _ = q.shape
    return pl.pallas_call(
        functools.partial(_flash_kernel, tq=tq, tk=tk, hp=hp),
        grid=(B, H // hp, S // tq),
        in_specs=[pl.BlockSpec((None, hp, tq, MLA_HEAD_PAD), lambda b, h, i: (b, h, i, 0)),
                  pl.BlockSpec((None, hp, S, MLA_HEAD_PAD), lambda b, h, i: (b, h, 0, 0)),
                  pl.BlockSpec((None, hp, S, MLA_V), lambda b, h, i: (b, h, 0, 0))],
        out_specs=pl.BlockSpec((None, tq, hp * MLA_V), lambda b, h, i: (b, i, h)),
        out_shape=jax.ShapeDtypeStruct((B, S, H * MLA_V), BF16),
        compiler_params=_params(("parallel", "parallel", "arbitrary")), name="mla_attention")(q, k, v)


def _pool_kernel(u_ref, w_ref, sc_ref, gn_ref, o_ref, buf_ref, *, ts):
    s = pl.program_id(1)

    @pl.when(s == 0)
    def _():
        buf_ref[0:POOL_HALO, :] = jnp.zeros((POOL_HALO, GROUP_WIDTH), F32)

    @pl.when(s > 0)
    def _():
        buf_ref[0:POOL_HALO, :] = buf_ref[ts:ts + POOL_HALO, :]

    buf_ref[POOL_HALO:POOL_HALO + ts, :] = u_ref[...].astype(F32)
    pos = s * ts + lax.broadcasted_iota(jnp.int32, (ts, 1), 0)
    ys = []
    ss = jnp.zeros((ts, 1), F32)
    for gi, w in enumerate(POOL_WINDOWS):
        lanes = slice(gi * POOL_GROUP, (gi + 1) * POOL_GROUP)
        ext = buf_ref[:, lanes]
        win = ext
        span = 1
        while span < w:
            win = win + pltpu.roll(win, span, 0)
            span *= 2
        cur = ext[POOL_HALO:, :]
        win = win[POOL_HALO:, :]
        count = jnp.minimum(pos + 1, w).astype(F32)
        d = win / count - cur
        y = jnp.dot(d.astype(BF16), w_ref[gi], preferred_element_type=F32) * sc_ref[:, lanes]
        ss = ss + jnp.sum(y * y, axis=-1, keepdims=True)
        ys.append(y)
    r = lax.rsqrt(ss / GROUP_WIDTH + EPS)
    for gi, y in enumerate(ys):
        lanes = slice(gi * POOL_GROUP, (gi + 1) * POOL_GROUP)
        o_ref[:, lanes] = (y * r * gn_ref[:, lanes]).astype(o_ref.dtype)


def _pool(u3, w_pool, pool_scale, gn_g, ts=512):
    B, S, _ = u3.shape
    return pl.pallas_call(
        functools.partial(_pool_kernel, ts=ts),
        grid=(B, S // ts),
        in_specs=[pl.BlockSpec((None, ts, GROUP_WIDTH), lambda b, s: (b, s, U_POOL // GROUP_WIDTH)),
                  pl.BlockSpec((len(POOL_WINDOWS), POOL_GROUP, POOL_GROUP), lambda b, s: (0, 0, 0)),
                  pl.BlockSpec((1, GROUP_WIDTH), lambda b, s: (0, 0)),
                  pl.BlockSpec((1, GROUP_WIDTH), lambda b, s: (0, 0))],
        out_specs=pl.BlockSpec((None, ts, GROUP_WIDTH), lambda b, s: (b, s, 0)),
        out_shape=jax.ShapeDtypeStruct((B, S, GROUP_WIDTH), BF16),
        scratch_shapes=[pltpu.VMEM((POOL_HALO + ts, GROUP_WIDTH), F32)],
        compiler_params=_params(("parallel", "arbitrary")), name="pool_mixer")(u3, w_pool, pool_scale, gn_g)


def _conv_kernel(a_ref, gate_ref, wdw_ref, bdw_ref, lng_ref, lnb_ref, wpw_ref, gn_ref, o_ref,
                 buf_ref, zc_ref, *, ts, rc):
    s = pl.program_id(1)

    @pl.when(s == 0)
    def _():
        buf_ref[0:CONV_HALO, :] = jnp.zeros((CONV_HALO, GROUP_WIDTH), F32)

    @pl.when(s > 0)
    def _():
        buf_ref[0:CONV_HALO, :] = buf_ref[ts:ts + CONV_HALO, :]

    a = a_ref[...].astype(F32)
    gate = gate_ref[...].astype(F32)
    buf_ref[CONV_HALO:CONV_HALO + ts, :] = a * jax.nn.sigmoid(gate)
    first = CONV_HALO - (CONV_WIDTH - 1)

    sub = 8
    n_win = rc + CONV_HALO

    def lane_block(c, _):
        lanes = pl.ds(pl.multiple_of(c * LANE, LANE), LANE)
        for r0 in range(0, ts, rc):
            acc = jnp.broadcast_to(bdw_ref[:, lanes], (rc, LANE))
            win = buf_ref[r0:r0 + n_win, lanes]
            for b in range(sub):
                taps = [j for j in range(CONV_WIDTH) if (first + j) % sub == b]
                if b == 0:
                    for j in taps:
                        acc = acc + buf_ref[r0 + first + j:r0 + first + j + rc, lanes] * wdw_ref[j:j + 1, lanes]
                else:
                    shifted = pltpu.roll(win, n_win - b, 0)
                    for j in taps:
                        a = (first + j - b)
                        acc = acc + shifted[a:a + rc, :] * wdw_ref[j:j + 1, lanes]
            zc_ref[r0:r0 + rc, lanes] = acc
        return 0

    lax.fori_loop(0, GROUP_WIDTH // LANE, lane_block, 0)
    z = zc_ref[...]
    mu = jnp.mean(z, axis=-1, keepdims=True)
    zc = z - mu
    zn = zc * lax.rsqrt(jnp.mean(zc * zc, axis=-1, keepdims=True) + EPS) * lng_ref[...] + lnb_ref[...]
    act = zn * jax.nn.sigmoid(zn)
    y = jnp.dot(act.astype(BF16), wpw_ref[...], preferred_element_type=F32)
    o_ref[...] = _rms(y, gn_ref[...]).astype(o_ref.dtype)


def _conv(u3, w_dw, b_dw, ln_g, ln_b, w_pw, gn_g, ts=512, rc=64):
    B, S, _ = u3.shape
    vec = pl.BlockSpec((1, GROUP_WIDTH), lambda b, s: (0, 0))
    return pl.pallas_call(
        functools.partial(_conv_kernel, ts=ts, rc=rc),
        grid=(B, S // ts),
        in_specs=[pl.BlockSpec((None, ts, GROUP_WIDTH), lambda b, s: (b, s, U_CONV_A // GROUP_WIDTH)),
                  pl.BlockSpec((None, ts, GROUP_WIDTH), lambda b, s: (b, s, U_CONV_G // GROUP_WIDTH)),
                  pl.BlockSpec((CONV_HALO, GROUP_WIDTH), lambda b, s: (0, 0)),
                  vec, vec, vec,
                  pl.BlockSpec((GROUP_WIDTH, GROUP_WIDTH), lambda b, s: (0, 0)),
                  vec],
        out_specs=pl.BlockSpec((None, ts, GROUP_WIDTH), lambda b, s: (b, s, 0)),
        out_shape=jax.ShapeDtypeStruct((B, S, GROUP_WIDTH), BF16),
        scratch_shapes=[pltpu.VMEM((CONV_HALO + ts, GROUP_WIDTH), F32), pltpu.VMEM((ts, GROUP_WIDTH), F32)],
        compiler_params=_params(("parallel", "arbitrary")), name="conv_mixer")(
            u3, u3, w_dw, b_dw, ln_g, ln_b, w_pw, gn_g)


def _swa_kernel(sink_ref, q_ref, kp_ref, kc_ref, vp_ref, vc_ref, gn_ref, o_ref):
    n = pl.program_id(1)
    W = SWA_WINDOW
    dh = SWA_HEAD_DIM
    R = SWA_Q_HEADS // SWA_KV_HEADS
    q = q_ref[...] * (dh ** -0.5)
    k2 = jnp.concatenate([kp_ref[...], kc_ref[...]], axis=0)
    v2 = jnp.concatenate([vp_ref[...], vc_ref[...]], axis=0)
    qi = lax.broadcasted_iota(jnp.int32, (W, 2 * W), 0)
    kj = lax.broadcasted_iota(jnp.int32, (W, 2 * W), 1)
    rel = qi + W - kj
    valid = (rel >= 0) & (rel < W) & (n * W + kj - W >= 0)
    outs = []
    for g in range(SWA_KV_HEADS):
        kg = k2[:, g * dh:(g + 1) * dh]
        vg = v2[:, g * dh:(g + 1) * dh]
        for r in range(R):
            h = g * R + r
            s = lax.dot_general(q[:, h * dh:(h + 1) * dh], kg, (((1,), (1,)), ((), ())),
                                preferred_element_type=F32)
            s = jnp.where(valid, s, NEG_INF)
            sink = sink_ref[h]
            m = jnp.maximum(jnp.max(s, axis=-1, keepdims=True), sink)
            e = jnp.exp(s - m)
            denom = jnp.sum(e, axis=-1, keepdims=True) + jnp.exp(sink - m)
            p = e / denom
            outs.append(jnp.dot(p.astype(BF16), vg, preferred_element_type=F32))
    y = jnp.concatenate(outs, axis=-1)
    o_ref[...] = _rms(y, gn_ref[...]).astype(o_ref.dtype)


def _swa(u3, sinks, gn_g):
    B, S, _ = u3.shape
    W = SWA_WINDOW
    kcol, vcol = U_SWA_K // LANE, U_SWA_V // LANE

    def prev(col):
        return pl.BlockSpec((None, W, LANE), lambda b, n, sk: (b, jnp.maximum(n - 1, 0), col))

    def cur(col):
        return pl.BlockSpec((None, W, LANE), lambda b, n, sk: (b, n, col))

    grid_spec = pltpu.PrefetchScalarGridSpec(
        num_scalar_prefetch=1, grid=(B, S // W),
        in_specs=[pl.BlockSpec((None, W, GROUP_WIDTH), lambda b, n, sk: (b, n, U_SWA_Q // GROUP_WIDTH)),
                  prev(kcol), cur(kcol), prev(vcol), cur(vcol),
                  pl.BlockSpec((1, GROUP_WIDTH), lambda b, n, sk: (0, 0))],
        out_specs=pl.BlockSpec((None, W, GROUP_WIDTH), lambda b, n, sk: (b, n, 0)))
    return pl.pallas_call(
        _swa_kernel, grid_spec=grid_spec,
        out_shape=jax.ShapeDtypeStruct((B, S, GROUP_WIDTH), BF16),
        compiler_params=_params(("parallel", "arbitrary")), name="swa_mixer")(
            sinks, u3, u3, u3, u3, u3, gn_g)


def _router_kernel(x_ref, g_ref, w_ref, o_ref, cnt_ref, carry_ref, *, tm):
    @pl.when(pl.program_id(0) == 0)
    def _():
        carry_ref[...] = jnp.zeros_like(carry_ref)

    h = _rms(x_ref[...], g_ref[...])
    logits = jnp.dot(h.astype(BF16), w_ref[...], preferred_element_type=F32)
    lane = lax.broadcasted_iota(jnp.int32, (tm, LANE), 1).astype(F32)
    logits = jnp.where(lane < N_EXPERTS, logits, -jnp.inf)
    m1 = jnp.max(logits, axis=-1, keepdims=True)
    i1 = jnp.min(jnp.where(logits == m1, lane, float(LANE)), axis=-1, keepdims=True)
    rest = jnp.where(lane == i1, -jnp.inf, logits)
    m2 = jnp.max(rest, axis=-1, keepdims=True)
    i2 = jnp.min(jnp.where(rest == m2, lane, float(LANE)), axis=-1, keepdims=True)
    e2 = jnp.exp(m2 - m1)
    w1 = 1.0 / (1.0 + e2)
    w2 = e2 / (1.0 + e2)
    oh1 = (lane == i1).astype(F32)
    oh2 = (lane == i2).astype(F32)
    cnt = oh1 + oh2
    row = lax.broadcasted_iota(jnp.int32, (tm, tm), 0)
    col = lax.broadcasted_iota(jnp.int32, (tm, tm), 1)
    before = (row > col).astype(BF16)
    pre = jnp.dot(before, cnt.astype(BF16), preferred_element_type=F32) + carry_ref[0:1, :]
    r1 = jnp.sum(oh1 * pre, axis=-1, keepdims=True)
    r2 = jnp.sum(oh2 * pre, axis=-1, keepdims=True)
    carry_ref[0:1, :] = carry_ref[0:1, :] + jnp.sum(cnt, axis=0, keepdims=True)
    cols = (i1, i2, r1, r2, w1, w2)
    out = jnp.zeros((tm, LANE), F32)
    for ci, val in enumerate(cols):
        out = jnp.where(lane == ci, val, out)
    o_ref[...] = out
    cnt_ref[...] = jnp.broadcast_to(carry_ref[0:1, :], cnt_ref.shape)


def _router(x, g, w_router_pad, tm=256):
    T, D = x.shape
    return pl.pallas_call(
        functools.partial(_router_kernel, tm=tm),
        grid=(T // tm,),
        in_specs=[pl.BlockSpec((tm, D), lambda i: (i, 0)),
                  pl.BlockSpec((1, D), lambda i: (0, 0)),
                  pl.BlockSpec((D, LANE), lambda i: (0, 0))],
        out_specs=[pl.BlockSpec((tm, LANE), lambda i: (i, 0)),
                   pl.BlockSpec((8, LANE), lambda i: (0, 0))],
        out_shape=[jax.ShapeDtypeStruct((T, LANE), F32), jax.ShapeDtypeStruct((8, LANE), F32)],
        scratch_shapes=[pltpu.VMEM((8, LANE), F32)],
        compiler_params=_params(("arbitrary",)), name="router")(x, g, w_router_pad)


def _row_copy(src_hbm, row, dst, slot, sem):
    return pltpu.make_async_copy(src_hbm.at[pl.ds(row, 1), :], dst.at[pl.ds(slot, 1), :], sem)


GATHER_UNROLL = 8
GATHER_AHEAD = 2
GATHER_SLOTS = GATHER_AHEAD + 1
NORM_CHUNK = 16


def _dispatch_kernel(tok_ref, ns_ref, x_hbm, g_ref, o_ref, buf_ref, sem, *, rows, per_tile, n_blocks):
    i = pl.program_id(0)
    ahead = i + GATHER_AHEAD
    n_chunks = rows // NORM_CHUNK

    def live(j):
        jc = jnp.minimum(j, n_blocks - 1)
        return (j < n_blocks) & (jc % per_tile < ns_ref[jc // per_tile])

    def copy(j, r):
        slot = j % GATHER_SLOTS
        return _row_copy(x_hbm, tok_ref[j * rows + r], buf_ref.at[slot], r, sem.at[slot])

    def start_block(j):
        def body(r, _):
            copy(j, r).start()
            return 0

        lax.fori_loop(0, rows, body, 0, unroll=GATHER_UNROLL)

    def wait_block(j):
        def body(r, _):
            copy(j, r).wait()
            return 0

        lax.fori_loop(0, rows, body, 0, unroll=GATHER_UNROLL)

    def norm_chunk(c):
        rs = slice(c * NORM_CHUNK, (c + 1) * NORM_CHUNK)
        o_ref[rs, :] = _rms(buf_ref[i % GATHER_SLOTS, rs, :], g_ref[...]).astype(o_ref.dtype)

    @pl.when(i == 0)
    def _():
        for j in range(GATHER_AHEAD):
            @pl.when(live(j))
            def _(j=j):
                start_block(j)

    cur, nxt = live(i), live(ahead)

    @pl.when(cur & nxt)
    def _():
        wait_block(i)
        for c in range(n_chunks):
            for r in range(c * NORM_CHUNK, (c + 1) * NORM_CHUNK):
                copy(ahead, r).start()
            norm_chunk(c)

    @pl.when(cur & jnp.logical_not(nxt))
    def _():
        wait_block(i)
        for c in range(n_chunks):
            norm_chunk(c)

    @pl.when(jnp.logical_not(cur))
    def _():
        o_ref[...] = jnp.zeros(o_ref.shape, o_ref.dtype)

    @pl.when(jnp.logical_not(cur) & nxt)
    def _():
        start_block(ahead)


def _dispatch(x, g, row_token, tile_nsub, n_rows):
    T, D = x.shape
    rows = MOE_SUB
    n_blocks = n_rows // rows
    grid_spec = pltpu.PrefetchScalarGridSpec(
        num_scalar_prefetch=2, grid=(n_blocks,),
        in_specs=[pl.BlockSpec(memory_space=pl.ANY),
                  pl.BlockSpec((1, D), lambda i, tok, ns: (0, 0))],
        out_specs=pl.BlockSpec((rows, D), lambda i, tok, ns: (i, 0)),
        scratch_shapes=[pltpu.VMEM((GATHER_SLOTS, rows, D), F32), pltpu.SemaphoreType.DMA((GATHER_SLOTS,))])
    return pl.pallas_call(
        functools.partial(_dispatch_kernel, rows=rows, per_tile=MOE_TILE // MOE_SUB, n_blocks=n_blocks),
        grid_spec=grid_spec, out_shape=jax.ShapeDtypeStruct((n_rows, D), BF16),
        compiler_params=_params(("arbitrary",)), name="moe_dispatch")(row_token, tile_nsub, x, g)


def _combine_kernel(p0_ref, p1_ref, x_ref, gate_ref, eo_hbm, g_ref, o_ref, a_ref, b_ref, sem, *, rows, n_blocks):
    i = pl.program_id(0)
    ahead = i + GATHER_AHEAD
    n_chunks = rows // NORM_CHUNK

    def copies(j, r):
        slot = j % GATHER_SLOTS
        return (_row_copy(eo_hbm, p0_ref[j * rows + r], a_ref.at[slot], r, sem.at[slot]),
                _row_copy(eo_hbm, p1_ref[j * rows + r], b_ref.at[slot], r, sem.at[slot]))

    def start_block(j):
        def body(r, _):
            for cp in copies(j, r):
                cp.start()
            return 0

        lax.fori_loop(0, rows, body, 0, unroll=GATHER_UNROLL // 2)

    def wait_block(j):
        def body(r, _):
            for cp in copies(j, r):
                cp.wait()
            return 0

        lax.fori_loop(0, rows, body, 0, unroll=GATHER_UNROLL // 2)

    def mix_chunk(c):
        slot = i % GATHER_SLOTS
        rs = slice(c * NORM_CHUNK, (c + 1) * NORM_CHUNK)
        gate = gate_ref[rs, :]
        y = x_ref[rs, :] + (gate[:, 0:1] * a_ref[slot, rs, :] + gate[:, 1:2] * b_ref[slot, rs, :])
        o_ref[rs, :] = _rms(y, g_ref[...]).astype(o_ref.dtype)

    @pl.when(i == 0)
    def _():
        for j in range(GATHER_AHEAD):
            start_block(j)

    wait_block(i)

    @pl.when(ahead < n_blocks)
    def _():
        for c in range(n_chunks):
            for r in range(c * NORM_CHUNK, (c + 1) * NORM_CHUNK):
                for cp in copies(ahead, r):
                    cp.start()
            mix_chunk(c)

    @pl.when(ahead >= n_blocks)
    def _():
        for c in range(n_chunks):
            mix_chunk(c)


def _combine(x, eo, pos0, pos1, gate, g, rows=128):
    T, D = x.shape
    n_blocks = T // rows
    grid_spec = pltpu.PrefetchScalarGridSpec(
        num_scalar_prefetch=2, grid=(n_blocks,),
        in_specs=[pl.BlockSpec((rows, D), lambda i, p0, p1: (i, 0)),
                  pl.BlockSpec((rows, 2), lambda i, p0, p1: (i, 0)),
                  pl.BlockSpec(memory_space=pl.ANY),
                  pl.BlockSpec((1, D), lambda i, p0, p1: (0, 0))],
        out_specs=pl.BlockSpec((rows, D), lambda i, p0, p1: (i, 0)),
        scratch_shapes=[pltpu.VMEM((GATHER_SLOTS, rows, D), F32), pltpu.VMEM((GATHER_SLOTS, rows, D), F32),
                        pltpu.SemaphoreType.DMA((GATHER_SLOTS,))])
    return pl.pallas_call(
        functools.partial(_combine_kernel, rows=rows, n_blocks=n_blocks),
        grid_spec=grid_spec, out_shape=jax.ShapeDtypeStruct((T, D), F32),
        compiler_params=_params(("arbitrary",)), name="moe_combine")(pos0, pos1, x, gate, eo, g)


W_IN_SEGMENTS = (
    (1600, U_POOL, 1024),
    (2624, U_CONV_A, 2048),
    (4672, U_SWA_Q, 1024),
    (0, U_CQ, 1536),
    (1536, U_KPE, 64),
    (5696, U_SWA_K, 256),
)


def _pack_kernel(w_ref, o_ref):
    o_ref[...] = jnp.zeros(o_ref.shape, o_ref.dtype)
    for src, dst, width in W_IN_SEGMENTS:
        o_ref[dst:dst + width, :] = w_ref[src:src + width, :].astype(o_ref.dtype)


def _pack_w_in(w_in_t, tc=256):
    L, W, D = w_in_t.shape
    return pl.pallas_call(
        _pack_kernel, grid=(L, D // tc),
        in_specs=[pl.BlockSpec((None, W, tc), lambda l, c: (l, 0, c))],
        out_specs=pl.BlockSpec((None, U_WIDTH, tc), lambda l, c: (l, 0, c)),
        out_shape=jax.ShapeDtypeStruct((L, U_WIDTH, D), BF16),
        compiler_params=_params(("parallel", "parallel")), name="pack_w_in")(w_in_t)


def _rope_tables(S):
    inv = 1.0 / (ROPE_THETA ** (jnp.arange(0, MLA_ROPE, 2, dtype=F32) / MLA_ROPE))
    ang = jnp.arange(S, dtype=F32)[:, None] * inv[None, :]
    cos, sin = jnp.cos(ang), jnp.sin(ang)
    z32 = jnp.zeros_like(cos)
    z64 = jnp.zeros((S, 64), F32)
    c = jnp.concatenate([cos, cos, z64], axis=-1)
    sa = jnp.concatenate([-sin, z32, z64], axis=-1)
    sb = jnp.concatenate([z32, sin, z64], axis=-1)
    return c, sa, sb


def _moe_plan(route, counts, T, n_tiles):
    E = N_EXPERTS
    expert = route[:, 0:2].astype(jnp.int32)
    rank = route[:, 2:4].astype(jnp.int32)
    gate = route[:, 4:6]
    counts = counts.astype(jnp.int32)
    tiles_per = (counts + MOE_TILE - 1) // MOE_TILE
    tile_end = jnp.cumsum(tiles_per)
    tile_start = tile_end - tiles_per
    used = tile_end[E - 1]
    even = (counts + jnp.maximum(tiles_per, 1) - 1) // jnp.maximum(tiles_per, 1)
    per = (even + MOE_SUB - 1) // MOE_SUB * MOE_SUB
    per = jnp.maximum(per, MOE_SUB)
    per_tok = per[expert]
    sub_tile = sum((rank >= m * per_tok).astype(jnp.int32) for m in range(1, pl.cdiv(T, MOE_TILE)))
    pos = (tile_start[expert] + sub_tile) * MOE_TILE + (rank - sub_tile * per_tok)
    n_rows = n_tiles * MOE_TILE
    flat = pos.reshape(-1)
    token = jnp.repeat(jnp.arange(T, dtype=jnp.int32), 2)
    row_token = jnp.zeros((n_rows,), jnp.int32).at[flat].set(token, unique_indices=True)
    t = jnp.arange(n_tiles, dtype=jnp.int32)
    te = jnp.minimum(jnp.sum((t[:, None] >= tile_end[None, :]).astype(jnp.int32), axis=1), E - 1)
    last = jnp.maximum(used - 1, 0)
    te = jnp.where(t < used, te, te[last])
    live_rows = jnp.clip(counts[te] - (t - tile_start[te]) * per[te], 0, per[te])
    nsub = jnp.where(t < used, (live_rows + MOE_SUB - 1) // MOE_SUB, 0).astype(jnp.int32)
    src = jnp.minimum(t, last)
    return pos[:, 0], pos[:, 1], gate, row_token, (te.astype(jnp.int32), nsub, src)


def kernel(x, attn_norm_g, w_in, mla_q_norm_g, mla_w_q_up, mla_kv_norm_g, mla_w_kv_up, pool_w, pool_scale,
           conv_w_dw, conv_b_dw, conv_ln_g, conv_ln_b, conv_w_pw, swa_sinks, group_out_g, w_out, ffn_norm_g,
           dense_w_gate, dense_w_up, dense_w_down, moe_w_router, moe_w_gate, moe_w_up, moe_w_down, final_norm_g):
    B, S, D = x.shape
    T = B * S
    L = w_in.shape[0]
    H = MLA_HEADS
    x = x.reshape(T, D)

    w_in_p = _pack_w_in(jnp.swapaxes(w_in, 1, 2))
    wq = jnp.pad(mla_w_q_up.reshape(L, MLA_Q_LORA, H, MLA_QK), ((0, 0), (0, 0), (0, 0), (0, MLA_HEAD_PAD - MLA_QK)))
    wq = wq.reshape(L, MLA_Q_LORA, H * MLA_HEAD_PAD).astype(BF16)
    wkv = mla_w_kv_up.astype(BF16)
    tabs = _rope_tables(S)
    pool_w_b = pool_w.astype(BF16)
    conv_w_pw_b = conv_w_pw.astype(BF16)
    conv_w_dw_p = jnp.pad(conv_w_dw, ((0, 0), (0, CONV_HALO - CONV_WIDTH), (0, 0)))
    gn = group_out_g.reshape(L, 4, 1, GROUP_WIDTH)
    row = lambda v: v.reshape(1, -1).astype(F32)

    assert L == 2, "layer 0 dense FFN, layer 1 (last) expert FFN"
    n_dense_tiles = T // DENSE_TM
    dense_tiles = (jnp.zeros((n_dense_tiles,), jnp.int32), jnp.ones((n_dense_tiles,), jnp.int32),
                   jnp.arange(n_dense_tiles, dtype=jnp.int32))

    delta = None
    out = None
    for l in range(L):
        if delta is None:
            h = _norm(x, attn_norm_g[l])
        else:
            x, h = _norm(x, attn_norm_g[l], delta=delta, write_sum=True)
        u = _mm_nt(h, w_in_p, l, out_dtype=BF16, **MM_IN)
        u3 = u.reshape(B, S, U_WIDTH)
        q, k, v = _mla_project(u, B, S, row(mla_q_norm_g[l]), wq[l], row(mla_kv_norm_g[l]), wkv[l], tabs)
        y_a = _mla_attention(q, k, v).reshape(T, GROUP_WIDTH)
        y_a = _norm(y_a, gn[l, 0], out_dtype=BF16, tm=1024)
        y_b = _pool(u3, pool_w_b[l], row(pool_scale[l]), gn[l, 1]).reshape(T, GROUP_WIDTH)
        y_c = _conv(u3, conv_w_dw_p[l], row(conv_b_dw[l]), row(conv_ln_g[l]), row(conv_ln_b[l]),
                    conv_w_pw_b[l], gn[l, 2]).reshape(T, GROUP_WIDTH)
        y_d = _swa(u3, swa_sinks[l].astype(F32), gn[l, 3]).reshape(T, GROUP_WIDTH)
        x = _mm([y_a, y_b, y_c, y_d], w_out, l, out_dtype=F32, res=x, **MM_OUT)
        i = l // 2
        if l % 2 == 0:
            h = _norm(x, ffn_norm_g[l])
            act = _swiglu(h, dense_w_gate, dense_w_up, dense_tiles, tm=DENSE_TM, tn=FFN_UP_TN, sub=DENSE_TM)
            delta = _down_dense(act, dense_w_down, i, out_dtype=BF16, **DOWN_DENSE)
        else:
            g = row(ffn_norm_g[l])
            w_r = jnp.pad(moe_w_router[i], ((0, 0), (0, LANE - N_EXPERTS))).astype(BF16)
            route, counts = _router(x, g, w_r)
            n_tiles = pl.cdiv(2 * T, MOE_TILE) + N_EXPERTS
            pos0, pos1, gate, row_token, tiles = _moe_plan(route, counts[0, :N_EXPERTS], T, n_tiles)
            xs = _dispatch(x, g, row_token, tiles[1], n_tiles * MOE_TILE)
            act = _swiglu(xs, moe_w_gate[i], moe_w_up[i], tiles, tm=MOE_TILE, tn=FFN_UP_TN, sub=MOE_SUB)
            eo = _down_grouped(act, moe_w_down[i], tiles, tm=MOE_TILE, tn=DOWN_GROUPED_TN, sub=MOE_SUB,
                               out_dtype=F32)
            out = _combine(x, eo, pos0, pos1, gate, row(final_norm_g))
    return out.reshape(B, S, D)
```

```python
import functools

import jax
import jax.numpy as jnp
from jax import lax
from jax.experimental import pallas as pl
from jax.experimental.pallas import tpu as pltpu

F32 = jnp.float32
BF16 = jnp.bfloat16
EPS = 1e-6
NEG_INF = -1e30
LOG2_E = 1.4426950408889634

GROUP_WIDTH = 1024
MLA_NOPE = 128
MLA_ROPE = 64
MLA_V = 128
MLA_HEADS = 8
MLA_QK = MLA_NOPE + MLA_ROPE
MLA_Q_LORA = 1024
MLA_KV_LORA = 512
MLA_HEAD_PAD = 256
ROPE_THETA = 10000.0
POOL_WINDOWS = (2, 4, 8, 16)
POOL_GROUP = 256
POOL_HALO = 16
CONV_WIDTH = 31
CONV_HALO = 32
SWA_HEAD_DIM = 64
SWA_Q_HEADS = 16
SWA_KV_HEADS = 2
SWA_WINDOW = 128
N_EXPERTS = 8

U_POOL = 0
U_CONV_A = 1024
U_CONV_G = 2048
U_SWA_Q = 3072
U_CQ = 4096
U_CKV = 5120
U_KPE = 5632
U_SWA_K = 5760
U_SWA_V = 5888
U_WIDTH = 6144

LANE = 128
V7X_VMEM_BYTES = 64 * 1024 * 1024
VMEM_LIMIT = V7X_VMEM_BYTES * 7 // 8

MOE_TILE = 1280
MOE_SUB = 128
DENSE_TM = 1024
FFN_UP_TN = 256
MM_IN = dict(tm=1024, tn=1024)
MM_OUT = dict(tm=1024, tn=512)
DOWN_DENSE = dict(tm=2048, tn=1024, tk=1536, tk_rem=256)
DOWN_GROUPED_TN = 256


def _params(sem, vmem=VMEM_LIMIT):
    return pltpu.CompilerParams(dimension_semantics=sem, vmem_limit_bytes=vmem)


def _rms(x, g):
    return x * lax.rsqrt(jnp.mean(x * x, axis=-1, keepdims=True) + EPS) * g


def _norm_kernel(*refs, has_delta, write_sum, out_dtype):
    it = iter(refs)
    x_ref = next(it)
    d_ref = next(it) if has_delta else None
    g_ref = next(it)
    s_ref = next(it) if write_sum else None
    o_ref = next(it)
    x = x_ref[...]
    if has_delta:
        x = x + d_ref[...].astype(F32)
    if write_sum:
        s_ref[...] = x
    o_ref[...] = _rms(x, g_ref[...]).astype(out_dtype)


def _norm(x, g, delta=None, write_sum=False, out_dtype=BF16, tm=256):
    T, D = x.shape
    row = pl.BlockSpec((tm, D), lambda i: (i, 0))
    in_specs = [row] + ([row] if delta is not None else []) + [pl.BlockSpec((1, D), lambda i: (0, 0))]
    args = [x] + ([delta] if delta is not None else []) + [g.reshape(1, D).astype(F32)]
    out_shape = [jax.ShapeDtypeStruct((T, D), out_dtype)]
    out_specs = [row]
    if write_sum:
        out_shape = [jax.ShapeDtypeStruct((T, D), F32)] + out_shape
        out_specs = [row] + out_specs
    res = pl.pallas_call(
        functools.partial(_norm_kernel, has_delta=delta is not None, write_sum=write_sum, out_dtype=out_dtype),
        grid=(T // tm,), in_specs=in_specs, out_specs=out_specs, out_shape=out_shape,
        compiler_params=_params(("parallel",)), name="norm")(*args)
    return res if write_sum else res[0]


def _mm_kernel(*refs, nx, has_res):
    x_refs = refs[:nx]
    w_ref = refs[nx]
    res_ref = refs[nx + 1] if has_res else None
    o_ref = refs[-1]
    acc = None
    off = 0
    for xr in x_refs:
        kx = xr.shape[1]
        p = jnp.dot(xr[...], w_ref[off:off + kx, :].astype(BF16), preferred_element_type=F32)
        acc = p if acc is None else acc + p
        off += kx
    if has_res:
        acc = acc + res_ref[...]
    o_ref[...] = acc.astype(o_ref.dtype)


def _mm(xs, w3, g, tm, tn, out_dtype, res=None):
    M = xs[0].shape[0]
    _, K, N = w3.shape
    assert sum(x.shape[1] for x in xs) == K
    in_specs = [pl.BlockSpec((tm, x.shape[1]), lambda m, n: (m, 0)) for x in xs]
    in_specs.append(pl.BlockSpec((None, K, tn), lambda m, n: (g, 0, n)))
    args = list(xs) + [w3]
    if res is not None:
        in_specs.append(pl.BlockSpec((tm, tn), lambda m, n: (m, n)))
        args.append(res)
    return pl.pallas_call(
        functools.partial(_mm_kernel, nx=len(xs), has_res=res is not None),
        grid=(M // tm, N // tn), in_specs=in_specs,
        out_specs=pl.BlockSpec((tm, tn), lambda m, n: (m, n)),
        out_shape=jax.ShapeDtypeStruct((M, N), out_dtype),
        compiler_params=_params(("parallel", "arbitrary")), name="mm")(*args)


def _mm_nt_kernel(x_ref, wt_ref, o_ref):
    o_ref[...] = lax.dot_general(x_ref[...], wt_ref[...], (((1,), (1,)), ((), ())),
                                 preferred_element_type=F32).astype(o_ref.dtype)


def _mm_nt(x, wt3, g, tm, tn, out_dtype):
    M, K = x.shape
    _, N, _ = wt3.shape
    return pl.pallas_call(
        _mm_nt_kernel, grid=(M // tm, N // tn),
        in_specs=[pl.BlockSpec((tm, K), lambda m, n: (m, 0)),
                  pl.BlockSpec((None, tn, K), lambda m, n: (g, n, 0))],
        out_specs=pl.BlockSpec((tm, tn), lambda m, n: (m, n)),
        out_shape=jax.ShapeDtypeStruct((M, N), out_dtype),
        compiler_params=_params(("parallel", "arbitrary")), name="mm_nt")(x, wt3)


def _swiglu_kernel(te_ref, ns_ref, src_ref, x_ref, wg_ref, wu_ref, o_ref, *, nsub, sub):
    ns = ns_ref[pl.program_id(0)]
    for v in range(nsub + 1):
        @pl.when(ns == v)
        def _(v=v):
            rows = v * sub
            if v > 0:
                x = x_ref[:rows, :]
                g = jnp.dot(x, wg_ref[...].astype(BF16), preferred_element_type=F32)
                u = jnp.dot(x, wu_ref[...].astype(BF16), preferred_element_type=F32)
                o_ref[:rows, :] = (g * jax.nn.sigmoid(g) * u).astype(o_ref.dtype)
            if v < nsub:
                o_ref[rows:, :] = jnp.zeros((nsub * sub - rows, o_ref.shape[1]), o_ref.dtype)


def _swiglu(x, wg, wu, tiles, tm, tn, sub):
    M, K = x.shape
    _, _, N = wg.shape
    n_n = pl.cdiv(N, tn)
    n_m = M // tm

    def x_map(m, n, te, ns, src):
        return (src[m], 0)

    def w_map(m, n, te, ns, src):
        return (te[m], 0, jnp.where(ns[m] > 0, n, n_n - 1))

    grid_spec = pltpu.PrefetchScalarGridSpec(
        num_scalar_prefetch=3, grid=(n_m, n_n),
        in_specs=[pl.BlockSpec((tm, K), x_map),
                  pl.BlockSpec((None, K, tn), w_map),
                  pl.BlockSpec((None, K, tn), w_map)],
        out_specs=pl.BlockSpec((tm, tn), lambda m, n, te, ns, src: (m, n)))
    return pl.pallas_call(
        functools.partial(_swiglu_kernel, nsub=tm // sub, sub=sub),
        grid_spec=grid_spec, out_shape=jax.ShapeDtypeStruct((M, N), BF16),
        compiler_params=_params(("parallel", "arbitrary")), name="swiglu_up")(*tiles, x, wg, wu)


def _down_dense_kernel(xm_ref, xr_ref, wm_ref, wr_ref, o_ref, acc_ref, *, n_main):
    k = pl.program_id(2)

    @pl.when(k == 0)
    def _():
        acc_ref[...] = jnp.dot(xm_ref[...], wm_ref[...].astype(BF16), preferred_element_type=F32)

    @pl.when((k > 0) & (k < n_main))
    def _():
        acc_ref[...] += jnp.dot(xm_ref[...], wm_ref[...].astype(BF16), preferred_element_type=F32)

    @pl.when(k == n_main)
    def _():
        o_ref[...] = (acc_ref[...] + jnp.dot(xr_ref[...], wr_ref[...].astype(BF16),
                                             preferred_element_type=F32)).astype(o_ref.dtype)


def _down_dense(x, w3, g, tm, tn, tk, tk_rem, out_dtype):
    M, K = x.shape
    _, _, N = w3.shape
    n_main = (K - tk_rem) // tk
    assert n_main * tk + tk_rem == K and (K - tk_rem) % tk_rem == 0
    rem_idx = (K - tk_rem) // tk_rem

    def km(k):
        return jnp.minimum(k, n_main - 1)

    return pl.pallas_call(
        functools.partial(_down_dense_kernel, n_main=n_main),
        grid=(M // tm, N // tn, n_main + 1),
        in_specs=[pl.BlockSpec((tm, tk), lambda m, n, k: (m, km(k))),
                  pl.BlockSpec((tm, tk_rem), lambda m, n, k: (m, rem_idx)),
                  pl.BlockSpec((None, tk, tn), lambda m, n, k: (g, km(k), n)),
                  pl.BlockSpec((None, tk_rem, tn), lambda m, n, k: (g, rem_idx, n))],
        out_specs=pl.BlockSpec((tm, tn), lambda m, n, k: (m, n)),
        out_shape=jax.ShapeDtypeStruct((M, N), out_dtype),
        scratch_shapes=[pltpu.VMEM((tm, tn), F32)],
        compiler_params=_params(("parallel", "arbitrary", "arbitrary")), name="down_dense")(x, x, w3, w3)


def _down_grouped_kernel(te_ref, ns_ref, src_ref, x_ref, w_ref, o_ref, *, nsub, sub):
    ns = ns_ref[pl.program_id(0)]
    for v in range(nsub + 1):
        @pl.when(ns == v)
        def _(v=v):
            rows = v * sub
            if v > 0:
                o_ref[:rows, :] = jnp.dot(x_ref[:rows, :], w_ref[...].astype(BF16),
                                          preferred_element_type=F32).astype(o_ref.dtype)
            if v < nsub:
                o_ref[rows:, :] = jnp.zeros((nsub * sub - rows, o_ref.shape[1]), o_ref.dtype)


def _down_grouped(x, w3, tiles, tm, tn, sub, out_dtype):
    M, K = x.shape
    _, _, N = w3.shape
    n_n = N // tn

    def w_map(m, n, te, ns, src):
        return (te[m], 0, jnp.where(ns[m] > 0, n, n_n - 1))

    grid_spec = pltpu.PrefetchScalarGridSpec(
        num_scalar_prefetch=3, grid=(M // tm, n_n),
        in_specs=[pl.BlockSpec((tm, K), lambda m, n, te, ns, src: (src[m], 0)),
                  pl.BlockSpec((None, K, tn), w_map)],
        out_specs=pl.BlockSpec((tm, tn), lambda m, n, te, ns, src: (m, n)))
    return pl.pallas_call(
        functools.partial(_down_grouped_kernel, nsub=tm // sub, sub=sub),
        grid_spec=grid_spec, out_shape=jax.ShapeDtypeStruct((M, N), out_dtype),
        compiler_params=_params(("parallel", "arbitrary")), name="down_grouped")(*tiles, x, w3)


def _rope(x, c, sa, sb):
    return x * c + pltpu.roll(x, LANE - MLA_ROPE // 2, 1) * sa + pltpu.roll(x, MLA_ROPE // 2, 1) * sb


def _mla_project_kernel(cq_ref, ckv_ref, kpe_ref, qg_ref, kvg_ref, wq_ref, wkv_ref, c_ref, sa_ref, sb_ref,
                        q_ref, k_ref, v_ref, *, scale):
    c, sa, sb = c_ref[...], sa_ref[...], sb_ref[...]
    xq = _rms(cq_ref[...].astype(F32), qg_ref[...]).astype(BF16)
    rq = jnp.dot(xq, wq_ref[...], preferred_element_type=F32)
    xkv = _rms(ckv_ref[...].astype(F32), kvg_ref[...]).astype(BF16)
    rkv = jnp.dot(xkv, wkv_ref[...], preferred_element_type=F32)
    kpe = _rope(kpe_ref[...].astype(F32), c, sa, sb).astype(k_ref.dtype)
    for h in range(MLA_HEADS):
        lo = h * MLA_HEAD_PAD
        mid = lo + MLA_NOPE
        hi = lo + MLA_HEAD_PAD
        q_ref[h, :, :MLA_NOPE] = (rq[:, lo:mid] * scale).astype(q_ref.dtype)
        q_ref[h, :, MLA_NOPE:] = (_rope(rq[:, mid:hi], c, sa, sb) * scale).astype(q_ref.dtype)
        k_ref[h, :, :MLA_NOPE] = rkv[:, lo:mid].astype(k_ref.dtype)
        k_ref[h, :, MLA_NOPE:] = kpe
        v_ref[h, :, :] = rkv[:, mid:hi].astype(v_ref.dtype)


def _mla_project(u, B, S, q_g, wq, kv_g, wkv, tabs, tm=512):
    T = B * S
    n_s = S // tm
    H = MLA_HEADS
    tab_spec = pl.BlockSpec((tm, LANE), lambda m: (m % n_s, 0))

    def const(shape):
        return pl.BlockSpec(shape, lambda m: (0,) * len(shape))

    def head_spec(width):
        return pl.BlockSpec((None, H, tm, width), lambda m: (m // n_s, 0, m % n_s, 0))

    return pl.pallas_call(
        functools.partial(_mla_project_kernel, scale=MLA_QK ** -0.5 * LOG2_E),
        grid=(T // tm,),
        in_specs=[pl.BlockSpec((tm, MLA_Q_LORA), lambda m: (m, U_CQ // MLA_Q_LORA)),
                  pl.BlockSpec((tm, MLA_KV_LORA), lambda m: (m, U_CKV // MLA_KV_LORA)),
                  pl.BlockSpec((tm, LANE), lambda m: (m, U_KPE // LANE)),
                  const((1, MLA_Q_LORA)), const((1, MLA_KV_LORA)),
                  const((MLA_Q_LORA, H * MLA_HEAD_PAD)), const((MLA_KV_LORA, H * MLA_HEAD_PAD)),
                  tab_spec, tab_spec, tab_spec],
        out_specs=[head_spec(MLA_HEAD_PAD), head_spec(MLA_HEAD_PAD), head_spec(MLA_V)],
        out_shape=[jax.ShapeDtypeStruct((B, H, S, MLA_HEAD_PAD), BF16),
                   jax.ShapeDtypeStruct((B, H, S, MLA_HEAD_PAD), BF16),
                   jax.ShapeDtypeStruct((B, H, S, MLA_V), BF16)],
        compiler_params=_params(("parallel",)), name="mla_project")(u, u, u, q_g, kv_g, wq, wkv, *tabs)


def _flash_kernel(q_ref, k_ref, v_ref, o_ref, *, tq, tk, hp):
    qi = pl.program_id(2)

    nk = tq // tk

    def step(j, carry, diag):
        start = pl.multiple_of(j * tk, tk)
        out = []
        for hh in range(hp):
            m, l, acc = carry[hh]
            s = lax.dot_general(q_ref[hh], k_ref[hh, pl.ds(start, tk), :], (((1,), (1,)), ((), ())),
                                preferred_element_type=F32)
            if diag is not None:
                row = lax.broadcasted_iota(jnp.int32, (tq, tk), 0)
                col = lax.broadcasted_iota(jnp.int32, (tq, tk), 1) + diag * tk
                s = jnp.where(row >= col, s, NEG_INF)
            m_new = jnp.maximum(m, jnp.max(s, axis=-1, keepdims=True))
            alpha = jnp.exp2(m - m_new)
            p = jnp.exp2(s - m_new)
            l = alpha * l + jnp.sum(p, axis=-1, keepdims=True)
            acc = alpha * acc + jnp.dot(p.astype(BF16), v_ref[hh, pl.ds(start, tk), :],
                                        preferred_element_type=F32)
            out.append((m_new, l, acc))
        return tuple(out)

    carry = tuple((jnp.full((tq, 1), NEG_INF, F32), jnp.zeros((tq, 1), F32), jnp.zeros((tq, MLA_V), F32))
                  for _ in range(hp))
    def full_tile(t, c):
        for d in range(nk):
            c = step(t * nk + d, c, None)
        return c

    carry = lax.fori_loop(0, qi, full_tile, carry)
    for d in range(nk):
        carry = step(qi * nk + d, carry, d)
    for hh in range(hp):
        _, l, acc = carry[hh]
        o_ref[:, hh * MLA_V:(hh + 1) * MLA_V] = (acc / l).astype(o_ref.dtype)


def _mla_attention(q, k, v, tq=1024, tk=512, hp=2):
    B, H, S, _ = q.shape
    return pl.pallas_call(
        functools.partial(_flash_kernel, tq=tq, tk=tk, hp=hp),
        grid=(B, H // hp, S // tq),
        in_specs=[pl.BlockSpec((None, hp, tq, MLA_HEAD_PAD), lambda b, h, i: (b, h, i, 0)),
                  pl.BlockSpec((None, hp, S, MLA_HEAD_PAD), lambda b, h, i: (b, h, 0, 0)),
                  pl.BlockSpec((None, hp, S, MLA_V), lambda b, h, i: (b, h, 0, 0))],
        out_specs=pl.BlockSpec((None, tq, hp * MLA_V), lambda b, h, i: (b, i, h)),
        out_shape=jax.ShapeDtypeStruct((B, S, H * MLA_V), BF16),
        compiler_params=_params(("parallel", "parallel", "arbitrary")), name="mla_attention")(q, k, v)


def _pool_kernel(u_ref, w_ref, sc_ref, gn_ref, o_ref, buf_ref, *, ts):
    s = pl.program_id(1)

    @pl.when(s == 0)
    def _():
        buf_ref[0:POOL_HALO, :] = jnp.zeros((POOL_HALO, GROUP_WIDTH), F32)

    @pl.when(s > 0)
    def _():
        buf_ref[0:POOL_HALO, :] = buf_ref[ts:ts + POOL_HALO, :]

    buf_ref[POOL_HALO:POOL_HALO + ts, :] = u_ref[...].astype(F32)
    pos = s * ts + lax.broadcasted_iota(jnp.int32, (ts, 1), 0)
    ys = []
    ss = jnp.zeros((ts, 1), F32)
    for gi, w in enumerate(POOL_WINDOWS):
        lanes = slice(gi * POOL_GROUP, (gi + 1) * POOL_GROUP)
        ext = buf_ref[:, lanes]
        win = ext
        span = 1
        while span < w:
            win = win + pltpu.roll(win, span, 0)
            span *= 2
        cur = ext[POOL_HALO:, :]
        win = win[POOL_HALO:, :]
        count = jnp.minimum(pos + 1, w).astype(F32)
        d = win / count - cur
        y = jnp.dot(d.astype(BF16), w_ref[gi], preferred_element_type=F32) * sc_ref[:, lanes]
        ss = ss + jnp.sum(y * y, axis=-1, keepdims=True)
        ys.append(y)
    r = lax.rsqrt(ss / GROUP_WIDTH + EPS)
    for gi, y in enumerate(ys):
        lanes = slice(gi * POOL_GROUP, (gi + 1) * POOL_GROUP)
        o_ref[:, lanes] = (y * r * gn_ref[:, lanes]).astype(o_ref.dtype)


def _pool(u3, w_pool, pool_scale, gn_g, ts=512):
    B, S, _ = u3.shape
    return pl.pallas_call(
        functools.partial(_pool_kernel, ts=ts),
        grid=(B, S // ts),
        in_specs=[pl.BlockSpec((None, ts, GROUP_WIDTH), lambda b, s: (b, s, U_POOL // GROUP_WIDTH)),
                  pl.BlockSpec((len(POOL_WINDOWS), POOL_GROUP, POOL_GROUP), lambda b, s: (0, 0, 0)),
                  pl.BlockSpec((1, GROUP_WIDTH), lambda b, s: (0, 0)),
                  pl.BlockSpec((1, GROUP_WIDTH), lambda b, s: (0, 0))],
        out_specs=pl.BlockSpec((None, ts, GROUP_WIDTH), lambda b, s: (b, s, 0)),
        out_shape=jax.ShapeDtypeStruct((B, S, GROUP_WIDTH), BF16),
        scratch_shapes=[pltpu.VMEM((POOL_HALO + ts, GROUP_WIDTH), F32)],
        compiler_params=_params(("parallel", "arbitrary")), name="pool_mixer")(u3, w_pool, pool_scale, gn_g)


def _conv_kernel(a_ref, gate_ref, wdw_ref, bdw_ref, lng_ref, lnb_ref, wpw_ref, gn_ref, o_ref,
                 buf_ref, zc_ref, *, ts, rc):
    s = pl.program_id(1)

    @pl.when(s == 0)
    def _():
        buf_ref[0:CONV_HALO, :] = jnp.zeros((CONV_HALO, GROUP_WIDTH), F32)

    @pl.when(s > 0)
    def _():
        buf_ref[0:CONV_HALO, :] = buf_ref[ts:ts + CONV_HALO, :]

    a = a_ref[...].astype(F32)
    gate = gate_ref[...].astype(F32)
    buf_ref[CONV_HALO:CONV_HALO + ts, :] = a * jax.nn.sigmoid(gate)
    first = CONV_HALO - (CONV_WIDTH - 1)

    sub = 8
    n_win = rc + CONV_HALO

    def lane_block(c, _):
        lanes = pl.ds(pl.multiple_of(c * LANE, LANE), LANE)
        for r0 in range(0, ts, rc):
            acc = jnp.broadcast_to(bdw_ref[:, lanes], (rc, LANE))
            win = buf_ref[r0:r0 + n_win, lanes]
            for b in range(sub):
                taps = [j for j in range(CONV_WIDTH) if (first + j) % sub == b]
                if b == 0:
                    for j in taps:
                        acc = acc + buf_ref[r0 + first + j:r0 + first + j + rc, lanes] * wdw_ref[j:j + 1, lanes]
                else:
                    shifted = pltpu.roll(win, n_win - b, 0)
                    for j in taps:
                        a = (first + j - b)
                        acc = acc + shifted[a:a + rc, :] * wdw_ref[j:j + 1, lanes]
            zc_ref[r0:r0 + rc, lanes] = acc
        return 0

    lax.fori_loop(0, GROUP_WIDTH // LANE, lane_block, 0)
    z = zc_ref[...]
    mu = jnp.mean(z, axis=-1, keepdims=True)
    zc = z - mu
    zn = zc * lax.rsqrt(jnp.mean(zc * zc, axis=-1, keepdims=True) + EPS) * lng_ref[...] + lnb_ref[...]
    act = zn * jax.nn.sigmoid(zn)
    y = jnp.dot(act.astype(BF16), wpw_ref[...], preferred_element_type=F32)
    o_ref[...] = _rms(y, gn_ref[...]).astype(o_ref.dtype)


def _conv(u3, w_dw, b_dw, ln_g, ln_b, w_pw, gn_g, ts=512, rc=64):
    B, S, _ = u3.shape
    vec = pl.BlockSpec((1, GROUP_WIDTH), lambda b, s: (0, 0))
    return pl.pallas_call(
        functools.partial(_conv_kernel, ts=ts, rc=rc),
        grid=(B, S // ts),
        in_specs=[pl.BlockSpec((None, ts, GROUP_WIDTH), lambda b, s: (b, s, U_CONV_A // GROUP_WIDTH)),
                  pl.BlockSpec((None, ts, GROUP_WIDTH), lambda b, s: (b, s, U_CONV_G // GROUP_WIDTH)),
                  pl.BlockSpec((CONV_HALO, GROUP_WIDTH), lambda b, s: (0, 0)),
                  vec, vec, vec,
                  pl.BlockSpec((GROUP_WIDTH, GROUP_WIDTH), lambda b, s: (0, 0)),
                  vec],
        out_specs=pl.BlockSpec((None, ts, GROUP_WIDTH), lambda b, s: (b, s, 0)),
        out_shape=jax.ShapeDtypeStruct((B, S, GROUP_WIDTH), BF16),
        scratch_shapes=[pltpu.VMEM((CONV_HALO + ts, GROUP_WIDTH), F32), pltpu.VMEM((ts, GROUP_WIDTH), F32)],
        compiler_params=_params(("parallel", "arbitrary")), name="conv_mixer")(
            u3, u3, w_dw, b_dw, ln_g, ln_b, w_pw, gn_g)


def _swa_kernel(sink_ref, q_ref, kp_ref, kc_ref, vp_ref, vc_ref, gn_ref, o_ref):
    n = pl.program_id(1)
    W = SWA_WINDOW
    dh = SWA_HEAD_DIM
    R = SWA_Q_HEADS // SWA_KV_HEADS
    q = q_ref[...] * (dh ** -0.5)
    k2 = jnp.concatenate([kp_ref[...], kc_ref[...]], axis=0)
    v2 = jnp.concatenate([vp_ref[...], vc_ref[...]], axis=0)
    qi = lax.broadcasted_iota(jnp.int32, (W, 2 * W), 0)
    kj = lax.broadcasted_iota(jnp.int32, (W, 2 * W), 1)
    rel = qi + W - kj
    valid = (rel >= 0) & (rel < W) & (n * W + kj - W >= 0)
    outs = []
    for g in range(SWA_KV_HEADS):
        kg = k2[:, g * dh:(g + 1) * dh]
        vg = v2[:, g * dh:(g + 1) * dh]
        for r in range(R):
            h = g * R + r
            s = lax.dot_general(q[:, h * dh:(h + 1) * dh], kg, (((1,), (1,)), ((), ())),
                                preferred_element_type=F32)
            s = jnp.where(valid, s, NEG_INF)
            sink = sink_ref[h]
            m = jnp.maximum(jnp.max(s, axis=-1, keepdims=True), sink)
            e = jnp.exp(s - m)
            denom = jnp.sum(e, axis=-1, keepdims=True) + jnp.exp(sink - m)
            p = e / denom
            outs.append(jnp.dot(p.astype(BF16), vg, preferred_element_type=F32))
    y = jnp.concatenate(outs, axis=-1)
    o_ref[...] = _rms(y, gn_ref[...]).astype(o_ref.dtype)


def _swa(u3, sinks, gn_g):
    B, S, _ = u3.shape
    W = SWA_WINDOW
    kcol, vcol = U_SWA_K // LANE, U_SWA_V // LANE

    def prev(col):
        return pl.BlockSpec((None, W, LANE), lambda b, n, sk: (b, jnp.maximum(n - 1, 0), col))

    def cur(col):
        return pl.BlockSpec((None, W, LANE), lambda b, n, sk: (b, n, col))

    grid_spec = pltpu.PrefetchScalarGridSpec(
        num_scalar_prefetch=1, grid=(B, S // W),
        in_specs=[pl.BlockSpec((None, W, GROUP_WIDTH), lambda b, n, sk: (b, n, U_SWA_Q // GROUP_WIDTH)),
                  prev(kcol), cur(kcol), prev(vcol), cur(vcol),
                  pl.BlockSpec((1, GROUP_WIDTH), lambda b, n, sk: (0, 0))],
        out_specs=pl.BlockSpec((None, W, GROUP_WIDTH), lambda b, n, sk: (b, n, 0)))
    return pl.pallas_call(
        _swa_kernel, grid_spec=grid_spec,
        out_shape=jax.ShapeDtypeStruct((B, S, GROUP_WIDTH), BF16),
        compiler_params=_params(("parallel", "arbitrary")), name="swa_mixer")(
            sinks, u3, u3, u3, u3, u3, gn_g)


def _router_kernel(x_ref, g_ref, w_ref, o_ref, cnt_ref, carry_ref, *, tm):
    @pl.when(pl.program_id(0) == 0)
    def _():
        carry_ref[...] = jnp.zeros_like(carry_ref)

    h = _rms(x_ref[...], g_ref[...])
    logits = jnp.dot(h.astype(BF16), w_ref[...], preferred_element_type=F32)
    lane = lax.broadcasted_iota(jnp.int32, (tm, LANE), 1).astype(F32)
    logits = jnp.where(lane < N_EXPERTS, logits, -jnp.inf)
    m1 = jnp.max(logits, axis=-1, keepdims=True)
    i1 = jnp.min(jnp.where(logits == m1, lane, float(LANE)), axis=-1, keepdims=True)
    rest = jnp.where(lane == i1, -jnp.inf, logits)
    m2 = jnp.max(rest, axis=-1, keepdims=True)
    i2 = jnp.min(jnp.where(rest == m2, lane, float(LANE)), axis=-1, keepdims=True)
    e2 = jnp.exp(m2 - m1)
    w1 = 1.0 / (1.0 + e2)
    w2 = e2 / (1.0 + e2)
    oh1 = (lane == i1).astype(F32)
    oh2 = (lane == i2).astype(F32)
    cnt = oh1 + oh2
    row = lax.broadcasted_iota(jnp.int32, (tm, tm), 0)
    col = lax.broadcasted_iota(jnp.int32, (tm, tm), 1)
    before = (row > col).astype(BF16)
    pre = jnp.dot(before, cnt.astype(BF16), preferred_element_type=F32) + carry_ref[0:1, :]
    r1 = jnp.sum(oh1 * pre, axis=-1, keepdims=True)
    r2 = jnp.sum(oh2 * pre, axis=-1, keepdims=True)
    carry_ref[0:1, :] = carry_ref[0:1, :] + jnp.sum(cnt, axis=0, keepdims=True)
    cols = (i1, i2, r1, r2, w1, w2)
    out = jnp.zeros((tm, LANE), F32)
    for ci, val in enumerate(cols):
        out = jnp.where(lane == ci, val, out)
    o_ref[...] = out
    cnt_ref[...] = jnp.broadcast_to(carry_ref[0:1, :], cnt_ref.shape)


def _router(x, g, w_router_pad, tm=256):
    T, D = x.shape
    return pl.pallas_call(
        functools.partial(_router_kernel, tm=tm),
        grid=(T // tm,),
        in_specs=[pl.BlockSpec((tm, D), lambda i: (i, 0)),
                  pl.BlockSpec((1, D), lambda i: (0, 0)),
                  pl.BlockSpec((D, LANE), lambda i: (0, 0))],
        out_specs=[pl.BlockSpec((tm, LANE), lambda i: (i, 0)),
                   pl.BlockSpec((8, LANE), lambda i: (0, 0))],
        out_shape=[jax.ShapeDtypeStruct((T, LANE), F32), jax.ShapeDtypeStruct((8, LANE), F32)],
        scratch_shapes=[pltpu.VMEM((8, LANE), F32)],
        compiler_params=_params(("arbitrary",)), name="router")(x, g, w_router_pad)


def _row_copy(src_hbm, row, dst, slot, sem):
    return pltpu.make_async_copy(src_hbm.at[pl.ds(row, 1), :], dst.at[pl.ds(slot, 1), :], sem)


GATHER_UNROLL = 8
GATHER_AHEAD = 2
DMA_PRIORITIES = 2
GATHER_SLOTS = GATHER_AHEAD + 1
NORM_CHUNK = 16


def _dispatch_kernel(tok_ref, ns_ref, x_hbm, g_ref, o_ref, buf_ref, sem, *, rows, per_tile, n_blocks):
    i = pl.program_id(0)
    ahead = i + GATHER_AHEAD
    n_chunks = rows // NORM_CHUNK

    def live(j):
        jc = jnp.minimum(j, n_blocks - 1)
        return (j < n_blocks) & (jc % per_tile < ns_ref[jc // per_tile])

    def copy(j, r):
        slot = j % GATHER_SLOTS
        return _row_copy(x_hbm, tok_ref[j * rows + r], buf_ref.at[slot], r, sem.at[slot])

    def start_block(j):
        def body(t, _):
            for p in range(DMA_PRIORITIES):
                copy(j, t * DMA_PRIORITIES + p).start(priority=p)
            return 0

        lax.fori_loop(0, rows // DMA_PRIORITIES, body, 0, unroll=GATHER_UNROLL // DMA_PRIORITIES)

    def wait_block(j):
        def body(r, _):
            copy(j, r).wait()
            return 0

        lax.fori_loop(0, rows, body, 0, unroll=GATHER_UNROLL)

    def norm_chunk(c):
        rs = slice(c * NORM_CHUNK, (c + 1) * NORM_CHUNK)
        o_ref[rs, :] = _rms(buf_ref[i % GATHER_SLOTS, rs, :], g_ref[...]).astype(o_ref.dtype)

    @pl.when(i == 0)
    def _():
        for j in range(GATHER_AHEAD):
            @pl.when(live(j))
            def _(j=j):
                start_block(j)

    cur, nxt = live(i), live(ahead)

    @pl.when(cur & nxt)
    def _():
        wait_block(i)
        for c in range(n_chunks):
            for r in range(c * NORM_CHUNK, (c + 1) * NORM_CHUNK):
                copy(ahead, r).start(priority=r % DMA_PRIORITIES)
            norm_chunk(c)

    @pl.when(cur & jnp.logical_not(nxt))
    def _():
        wait_block(i)
        for c in range(n_chunks):
            norm_chunk(c)

    @pl.when(jnp.logical_not(cur))
    def _():
        o_ref[...] = jnp.zeros(o_ref.shape, o_ref.dtype)

    @pl.when(jnp.logical_not(cur) & nxt)
    def _():
        start_block(ahead)


def _dispatch(x, g, row_token, tile_nsub, n_rows):
    T, D = x.shape
    rows = MOE_SUB
    n_blocks = n_rows // rows
    grid_spec = pltpu.PrefetchScalarGridSpec(
        num_scalar_prefetch=2, grid=(n_blocks,),
        in_specs=[pl.BlockSpec(memory_space=pl.ANY),
                  pl.BlockSpec((1, D), lambda i, tok, ns: (0, 0))],
        out_specs=pl.BlockSpec((rows, D), lambda i, tok, ns: (i, 0)),
        scratch_shapes=[pltpu.VMEM((GATHER_SLOTS, rows, D), F32), pltpu.SemaphoreType.DMA((GATHER_SLOTS,))])
    return pl.pallas_call(
        functools.partial(_dispatch_kernel, rows=rows, per_tile=MOE_TILE // MOE_SUB, n_blocks=n_blocks),
        grid_spec=grid_spec, out_shape=jax.ShapeDtypeStruct((n_rows, D), BF16),
        compiler_params=_params(("arbitrary",)), name="moe_dispatch")(row_token, tile_nsub, x, g)


def _combine_kernel(p0_ref, p1_ref, x_ref, gate_ref, eo_hbm, g_ref, o_ref, a_ref, b_ref, sem, *, rows, n_blocks):
    i = pl.program_id(0)
    ahead = i + GATHER_AHEAD
    n_chunks = rows // NORM_CHUNK

    def copies(j, r):
        slot = j % GATHER_SLOTS
        return (_row_copy(eo_hbm, p0_ref[j * rows + r], a_ref.at[slot], r, sem.at[slot]),
                _row_copy(eo_hbm, p1_ref[j * rows + r], b_ref.at[slot], r, sem.at[slot]))

    def start_block(j):
        def body(r, _):
            for p, cp in enumerate(copies(j, r)):
                cp.start(priority=p % DMA_PRIORITIES)
            return 0

        lax.fori_loop(0, rows, body, 0, unroll=GATHER_UNROLL // 2)

    def wait_block(j):
        def body(r, _):
            for cp in copies(j, r):
                cp.wait()
            return 0

        lax.fori_loop(0, rows, body, 0, unroll=GATHER_UNROLL // 2)

    def mix_chunk(c):
        slot = i % GATHER_SLOTS
        rs = slice(c * NORM_CHUNK, (c + 1) * NORM_CHUNK)
        gate = gate_ref[rs, :]
        y = x_ref[rs, :] + (gate[:, 0:1] * a_ref[slot, rs, :] + gate[:, 1:2] * b_ref[slot, rs, :])
        o_ref[rs, :] = _rms(y, g_ref[...]).astype(o_ref.dtype)

    @pl.when(i == 0)
    def _():
        for j in range(GATHER_AHEAD):
            start_block(j)

    wait_block(i)

    @pl.when(ahead < n_blocks)
    def _():
        for c in range(n_chunks):
            for r in range(c * NORM_CHUNK, (c + 1) * NORM_CHUNK):
                for p, cp in enumerate(copies(ahead, r)):
                    cp.start(priority=p % DMA_PRIORITIES)
            mix_chunk(c)

    @pl.when(ahead >= n_blocks)
    def _():
        for c in range(n_chunks):
            mix_chunk(c)


def _combine(x, eo, pos0, pos1, gate, g, rows=128):
    T, D = x.shape
    n_blocks = T // rows
    grid_spec = pltpu.PrefetchScalarGridSpec(
        num_scalar_prefetch=2, grid=(n_blocks,),
        in_specs=[pl.BlockSpec((rows, D), lambda i, p0, p1: (i, 0)),
                  pl.BlockSpec((rows, 2), lambda i, p0, p1: (i, 0)),
                  pl.BlockSpec(memory_space=pl.ANY),
                  pl.BlockSpec((1, D), lambda i, p0, p1: (0, 0))],
        out_specs=pl.BlockSpec((rows, D), lambda i, p0, p1: (i, 0)),
        scratch_shapes=[pltpu.VMEM((GATHER_SLOTS, rows, D), F32), pltpu.VMEM((GATHER_SLOTS, rows, D), F32),
                        pltpu.SemaphoreType.DMA((GATHER_SLOTS,))])
    return pl.pallas_call(
        functools.partial(_combine_kernel, rows=rows, n_blocks=n_blocks),
        grid_spec=grid_spec, out_shape=jax.ShapeDtypeStruct((T, D), F32),
        compiler_params=_params(("arbitrary",)), name="moe_combine")(pos0, pos1, x, gate, eo, g)


W_IN_SEGMENTS = (
    (1600, U_POOL, 1024),
    (2624, U_CONV_A, 2048),
    (4672, U_SWA_Q, 1024),
    (0, U_CQ, 1536),
    (1536, U_KPE, 64),
    (5696, U_SWA_K, 256),
)


def _pack_kernel(w_ref, o_ref):
    o_ref[...] = jnp.zeros(o_ref.shape, o_ref.dtype)
    for src, dst, width in W_IN_SEGMENTS:
        o_ref[dst:dst + width, :] = w_ref[src:src + width, :].astype(o_ref.dtype)


def _pack_w_in(w_in_t, tc=256):
    L, W, D = w_in_t.shape
    return pl.pallas_call(
        _pack_kernel, grid=(L, D // tc),
        in_specs=[pl.BlockSpec((None, W, tc), lambda l, c: (l, 0, c))],
        out_specs=pl.BlockSpec((None, U_WIDTH, tc), lambda l, c: (l, 0, c)),
        out_shape=jax.ShapeDtypeStruct((L, U_WIDTH, D), BF16),
        compiler_params=_params(("parallel", "parallel")), name="pack_w_in")(w_in_t)


def _rope_tables(S):
    inv = 1.0 / (ROPE_THETA ** (jnp.arange(0, MLA_ROPE, 2, dtype=F32) / MLA_ROPE))
    ang = jnp.arange(S, dtype=F32)[:, None] * inv[None, :]
    cos, sin = jnp.cos(ang), jnp.sin(ang)
    z32 = jnp.zeros_like(cos)
    z64 = jnp.zeros((S, 64), F32)
    c = jnp.concatenate([cos, cos, z64], axis=-1)
    sa = jnp.concatenate([-sin, z32, z64], axis=-1)
    sb = jnp.concatenate([z32, sin, z64], axis=-1)
    return c, sa, sb


def _moe_plan(route, counts, T, n_tiles):
    E = N_EXPERTS
    expert = route[:, 0:2].astype(jnp.int32)
    rank = route[:, 2:4].astype(jnp.int32)
    gate = route[:, 4:6]
    counts = counts.astype(jnp.int32)
    tiles_per = (counts + MOE_TILE - 1) // MOE_TILE
    tile_end = jnp.cumsum(tiles_per)
    tile_start = tile_end - tiles_per
    used = tile_end[E - 1]
    even = (counts + jnp.maximum(tiles_per, 1) - 1) // jnp.maximum(tiles_per, 1)
    per = (even + MOE_SUB - 1) // MOE_SUB * MOE_SUB
    per = jnp.maximum(per, MOE_SUB)
    per_tok = per[expert]
    sub_tile = sum((rank >= m * per_tok).astype(jnp.int32) for m in range(1, pl.cdiv(T, MOE_TILE)))
    pos = (tile_start[expert] + sub_tile) * MOE_TILE + (rank - sub_tile * per_tok)
    n_rows = n_tiles * MOE_TILE
    flat = pos.reshape(-1)
    token = jnp.repeat(jnp.arange(T, dtype=jnp.int32), 2)
    row_token = jnp.zeros((n_rows,), jnp.int32).at[flat].set(token, unique_indices=True)
    t = jnp.arange(n_tiles, dtype=jnp.int32)
    te = jnp.minimum(jnp.sum((t[:, None] >= tile_end[None, :]).astype(jnp.int32), axis=1), E - 1)
    last = jnp.maximum(used - 1, 0)
    te = jnp.where(t < used, te, te[last])
    live_rows = jnp.clip(counts[te] - (t - tile_start[te]) * per[te], 0, per[te])
    nsub = jnp.where(t < used, (live_rows + MOE_SUB - 1) // MOE_SUB, 0).astype(jnp.int32)
    src = jnp.minimum(t, last)
    return pos[:, 0], pos[:, 1], gate, row_token, (te.astype(jnp.int32), nsub, src)


def kernel(x, attn_norm_g, w_in, mla_q_norm_g, mla_w_q_up, mla_kv_norm_g, mla_w_kv_up, pool_w, pool_scale,
           conv_w_dw, conv_b_dw, conv_ln_g, conv_ln_b, conv_w_pw, swa_sinks, group_out_g, w_out, ffn_norm_g,
           dense_w_gate, dense_w_up, dense_w_down, moe_w_router, moe_w_gate, moe_w_up, moe_w_down, final_norm_g):
    B, S, D = x.shape
    T = B * S
    L = w_in.shape[0]
    H = MLA_HEADS
    x = x.reshape(T, D)

    w_in_p = _pack_w_in(jnp.swapaxes(w_in, 1, 2))
    wq = jnp.pad(mla_w_q_up.reshape(L, MLA_Q_LORA, H, MLA_QK), ((0, 0), (0, 0), (0, 0), (0, MLA_HEAD_PAD - MLA_QK)))
    wq = wq.reshape(L, MLA_Q_LORA, H * MLA_HEAD_PAD).astype(BF16)
    wkv = mla_w_kv_up.astype(BF16)
    tabs = _rope_tables(S)
    pool_w_b = pool_w.astype(BF16)
    conv_w_pw_b = conv_w_pw.astype(BF16)
    conv_w_dw_p = jnp.pad(conv_w_dw, ((0, 0), (0, CONV_HALO - CONV_WIDTH), (0, 0)))
    gn = group_out_g.reshape(L, 4, 1, GROUP_WIDTH)
    row = lambda v: v.reshape(1, -1).astype(F32)

    assert L == 2, "layer 0 dense FFN, layer 1 (last) expert FFN"
    n_dense_tiles = T // DENSE_TM
    dense_tiles = (jnp.zeros((n_dense_tiles,), jnp.int32), jnp.ones((n_dense_tiles,), jnp.int32),
                   jnp.arange(n_dense_tiles, dtype=jnp.int32))

    delta = None
    out = None
    for l in range(L):
        if delta is None:
            h = _norm(x, attn_norm_g[l])
        else:
            x, h = _norm(x, attn_norm_g[l], delta=delta, write_sum=True)
        u = _mm_nt(h, w_in_p, l, out_dtype=BF16, **MM_IN)
        u3 = u.reshape(B, S, U_WIDTH)
        q, k, v = _mla_project(u, B, S, row(mla_q_norm_g[l]), wq[l], row(mla_kv_norm_g[l]), wkv[l], tabs)
        y_a = _mla_attention(q, k, v).reshape(T, GROUP_WIDTH)
        y_a = _norm(y_a, gn[l, 0], out_dtype=BF16, tm=1024)
        y_b = _pool(u3, pool_w_b[l], row(pool_scale[l]), gn[l, 1]).reshape(T, GROUP_WIDTH)
        y_c = _conv(u3, conv_w_dw_p[l], row(conv_b_dw[l]), row(conv_ln_g[l]), row(conv_ln_b[l]),
                    conv_w_pw_b[l], gn[l, 2]).reshape(T, GROUP_WIDTH)
        y_d = _swa(u3, swa_sinks[l].astype(F32), gn[l, 3]).reshape(T, GROUP_WIDTH)
        x = _mm([y_a, y_b, y_c, y_d], w_out, l, out_dtype=F32, res=x, **MM_OUT)
        i = l // 2
        if l % 2 == 0:
            h = _norm(x, ffn_norm_g[l])
            act = _swiglu(h, dense_w_gate, dense_w_up, dense_tiles, tm=DENSE_TM, tn=FFN_UP_TN, sub=DENSE_TM)
            delta = _down_dense(act, dense_w_down, i, out_dtype=BF16, **DOWN_DENSE)
        else:
            g = row(ffn_norm_g[l])
            w_r = jnp.pad(moe_w_router[i], ((0, 0), (0, LANE - N_EXPERTS))).astype(BF16)
            route, counts = _router(x, g, w_r)
            n_tiles = pl.cdiv(2 * T, MOE_TILE) + N_EXPERTS
            pos0, pos1, gate, row_token, tiles = _moe_plan(route, counts[0, :N_EXPERTS], T, n_tiles)
            xs = _dispatch(x, g, row_token, tiles[1], n_tiles * MOE_TILE)
            act = _swiglu(xs, moe_w_gate[i], moe_w_up[i], tiles, tm=MOE_TILE, tn=FFN_UP_TN, sub=MOE_SUB)
            eo = _down_grouped(act, moe_w_down[i], tiles, tm=MOE_TILE, tn=DOWN_GROUPED_TN, sub=MOE_SUB,
                               out_dtype=F32)
            out = _combine(x, eo, pos0, pos1, gate, row(final_norm_g))
    return out.reshape(B, S, D)
```
